```python
import jax, jax.numpy as jnp
from jax import lax
import numpy as np

D_MODEL = 2048
BATCH = 16
SEQ = 2048
DEPTH = 1
DEC_BATCH = 4
DEC_SEQ = 2048
PAST_LEN = 128

D_FF = 5632
D_CONV = 1024
CONV_WIDTH = 3
N_GROUPS = 3
HEADS_PER_GROUP = 4
N_HEADS = N_GROUPS * HEADS_PER_GROUP
HEAD_DIM = 128
D_ATTN = HEADS_PER_GROUP * HEAD_DIM
D_QKV = N_GROUPS * D_ATTN
ATTN_WINDOWS = (128, 512, 2048)
ATTN_DILATIONS = (1, 4, 16)
N_BRANCHES = 2
D_IN = 3 * D_CONV + 3 * D_QKV
RMS_EPS = 1e-6
FFN_RESIDUAL_WEIGHT = 0.5

kernel_name = "hybrid_conv_dilated_attn_macaron_encoder"


def rms_norm(x, g):
    xf = x.astype(jnp.float32)
    y = xf * lax.rsqrt(jnp.mean(xf * xf, axis=-1, keepdims=True) + RMS_EPS)
    return (y * g.astype(jnp.float32)).astype(x.dtype)


def swiglu(x, w1, w3, w2):
    a = jnp.einsum('bsd,df->bsf', x, w1)
    c = jnp.einsum('bsd,df->bsf', x, w3)
    return jnp.einsum('bsf,fd->bsd', jax.nn.silu(a) * c, w2)


def alibi_slopes():
    i = jnp.arange(1, N_HEADS + 1, dtype=jnp.float32)
    return jnp.exp2(-8.0 * i / N_HEADS).reshape(N_GROUPS, HEADS_PER_GROUP)


def dilated_band_attention(q, k, v, dilation, radius, slopes):
    b, s, h, hd = q.shape
    L = s // dilation

    def to_sub(t):
        return t.reshape(b, L, dilation, h, hd).transpose(0, 2, 1, 3, 4).reshape(b * dilation, L, h, hd)

    qs, ks, vs = to_sub(q), to_sub(k), to_sub(v)
    blk = radius
    nb = -(-L // blk)
    lp = nb * blk
    qb = jnp.pad(qs, ((0, 0), (0, lp - L), (0, 0), (0, 0))).reshape(-1, nb, blk, h, hd)

    def windows(t):
        tb = jnp.pad(t, ((0, 0), (blk, lp - L + blk), (0, 0), (0, 0))).reshape(-1, nb + 2, blk, h, hd)
        return jnp.concatenate([tb[:, :-2], tb[:, 1:-1], tb[:, 2:]], axis=2)

    kw, vw = windows(ks), windows(vs)
    scores = jnp.einsum('nbqhd,nbkhd->nbhqk', qb, kw).astype(jnp.float32) * (HEAD_DIM ** -0.5)
    q_pos = jnp.arange(nb)[:, None] * blk + jnp.arange(blk)[None, :]
    k_pos = (jnp.arange(nb)[:, None] - 1) * blk + jnp.arange(3 * blk)[None, :]
    delta = jnp.abs(k_pos[:, None, :] - q_pos[:, :, None])
    valid = (delta <= radius) & (k_pos[:, None, :] >= 0) & (k_pos[:, None, :] < L)
    dist = (dilation * delta).astype(jnp.float32)
    bias = -slopes.astype(jnp.float32)[None, :, None, None] * dist[:, None]
    scores = jnp.where(valid[:, None][None], scores + bias[None], -jnp.inf)
    m = jnp.max(scores, axis=-1, keepdims=True)
    p = jnp.exp(scores - m)
    l = jnp.sum(p, axis=-1, keepdims=True)
    o = jnp.einsum('nbhqk,nbkhd->nbqhd', (p / l).astype(v.dtype), vw)
    lse = (m + jnp.log(l))[..., 0]
    o = o.reshape(b, dilation, lp, h, hd)[:, :, :L].transpose(0, 2, 1, 3, 4).reshape(b, s, h, hd)
    lse = lse.transpose(0, 1, 3, 2).reshape(b, dilation, lp, h)[:, :, :L].transpose(0, 2, 1, 3).reshape(b, s, h)
    return o, lse


def token_mixer(u, w_in, conv_w, conv_b, w_conv_out, w_attn_out, w_gate, b_gate, w_o):
    b, s, _ = u.shape
    proj = jnp.einsum('bsd,de->bse', u, w_in)
    c_b, c_c, c_x, q, k, v = jnp.split(
        proj, [D_CONV, 2 * D_CONV, 3 * D_CONV, 3 * D_CONV + D_QKV, 3 * D_CONV + 2 * D_QKV], axis=-1)

    z = c_c * c_x
    half = CONV_WIDTH // 2
    zp = jnp.pad(z, ((0, 0), (half, half), (0, 0)))
    conv = sum(zp[:, j:j + s] * conv_w[j] for j in range(CONV_WIDTH)) + conv_b
    y_conv = jnp.einsum('bsc,cd->bsd', c_b * conv, w_conv_out)

    q = q.reshape(b, s, N_GROUPS, HEADS_PER_GROUP, HEAD_DIM)
    k = k.reshape(b, s, N_GROUPS, HEADS_PER_GROUP, HEAD_DIM)
    v = v.reshape(b, s, N_GROUPS, HEADS_PER_GROUP, HEAD_DIM)
    slopes = alibi_slopes()
    outs, lses = [], []
    for g in range(N_GROUPS):
        d = ATTN_DILATIONS[g]
        o_g, lse_g = dilated_band_attention(q[:, :, g], k[:, :, g], v[:, :, g], d,
                                            ATTN_WINDOWS[g] // (2 * d), slopes[g])
        outs.append(o_g)
        lses.append(lse_g)
    alpha = jax.nn.softmax(jnp.stack(lses, axis=0), axis=0)
    o = jnp.sum(alpha[..., None].astype(v.dtype) * jnp.stack(outs, axis=0), axis=0).reshape(b, s, D_ATTN)
    y_attn = jnp.einsum('bse,ed->bsd', o, w_attn_out)

    gates = jax.nn.sigmoid(jnp.einsum('bsd,de->bse', u, w_gate) + b_gate).reshape(b, s, N_BRANCHES, D_MODEL)
    merged = gates[:, :, 0] * y_conv + gates[:, :, 1] * y_attn
    return jnp.einsum('bsd,de->bse', merged, w_o)


def encoder_layer(x, ffn1_norm_pre, ffn1_w1, ffn1_w3, ffn1_w2, ffn1_norm_post,
                  mix_norm_pre, w_in, conv_w, conv_b, w_conv_out, w_attn_out, w_gate, b_gate, w_o,
                  mix_norm_post, ffn2_norm_pre, ffn2_w1, ffn2_w3, ffn2_w2, ffn2_norm_post):
    h = x + FFN_RESIDUAL_WEIGHT * rms_norm(swiglu(rms_norm(x, ffn1_norm_pre), ffn1_w1, ffn1_w3, ffn1_w2), ffn1_norm_post)
    mix = token_mixer(rms_norm(h, mix_norm_pre), w_in, conv_w, conv_b, w_conv_out, w_attn_out, w_gate, b_gate, w_o)
    h = h + rms_norm(mix, mix_norm_post)
    return h + FFN_RESIDUAL_WEIGHT * rms_norm(swiglu(rms_norm(h, ffn2_norm_pre), ffn2_w1, ffn2_w3, ffn2_w2), ffn2_norm_post)


def setup_inputs(seed: int = 0) -> dict:
    key = jax.random.key(seed)
    ks = jax.random.split(key, 24)

    def nrm(k, shape, fan_in):
        return jax.random.normal(k, shape, jnp.float32) * (fan_in ** -0.5)

    def gain(k):
        return 1.0 + 0.05 * jax.random.normal(k, (DEPTH, D_MODEL), jnp.float32)

    return {
        "x_prompt": jax.random.normal(ks[0], (BATCH, SEQ, D_MODEL), jnp.float32),
        "x_sample": jax.random.normal(ks[1], (DEC_BATCH, DEC_SEQ, D_MODEL), jnp.float32),
        "ffn1_norm_pre": gain(ks[2]),
        "ffn1_w1": nrm(ks[3], (DEPTH, D_MODEL, D_FF), D_MODEL),
        "ffn1_w3": nrm(ks[4], (DEPTH, D_MODEL, D_FF), D_MODEL),
        "ffn1_w2": nrm(ks[5], (DEPTH, D_FF, D_MODEL), D_FF),
        "ffn1_norm_post": gain(ks[6]),
        "mix_norm_pre": gain(ks[7]),
        "w_in": nrm(ks[8], (DEPTH, D_MODEL, D_IN), D_MODEL),
        "conv_w": nrm(ks[9], (DEPTH, CONV_WIDTH, D_CONV), CONV_WIDTH),
        "conv_b": 0.02 * jax.random.normal(ks[10], (DEPTH, D_CONV), jnp.float32),
        "w_conv_out": nrm(ks[11], (DEPTH, D_CONV, D_MODEL), D_CONV),
        "w_attn_out": nrm(ks[12], (DEPTH, D_ATTN, D_MODEL), D_ATTN),
        "w_gate": nrm(ks[13], (DEPTH, D_MODEL, N_BRANCHES * D_MODEL), D_MODEL),
        "b_gate": 0.1 * jax.random.normal(ks[14], (DEPTH, N_BRANCHES * D_MODEL), jnp.float32),
        "w_o": nrm(ks[15], (DEPTH, D_MODEL, D_MODEL), D_MODEL),
        "mix_norm_post": gain(ks[16]),
        "ffn2_norm_pre": gain(ks[17]),
        "ffn2_w1": nrm(ks[18], (DEPTH, D_MODEL, D_FF), D_MODEL),
        "ffn2_w3": nrm(ks[19], (DEPTH, D_MODEL, D_FF), D_MODEL),
        "ffn2_w2": nrm(ks[20], (DEPTH, D_FF, D_MODEL), D_FF),
        "ffn2_norm_post": gain(ks[21]),
    }


def reference(x_prompt, x_sample, ffn1_norm_pre, ffn1_w1, ffn1_w3, ffn1_w2, ffn1_norm_post,
              mix_norm_pre, w_in, conv_w, conv_b, w_conv_out, w_attn_out, w_gate, b_gate, w_o,
              mix_norm_post, ffn2_norm_pre, ffn2_w1, ffn2_w3, ffn2_w2, ffn2_norm_post):
    weights = (ffn1_norm_pre, ffn1_w1, ffn1_w3, ffn1_w2, ffn1_norm_post,
               mix_norm_pre, w_in, conv_w, conv_b, w_conv_out, w_attn_out, w_gate, b_gate, w_o,
               mix_norm_post, ffn2_norm_pre, ffn2_w1, ffn2_w3, ffn2_w2, ffn2_norm_post)
    y_prompt = x_prompt
    y_sample = x_sample
    for layer in range(DEPTH):
        lw = [w[layer] for w in weights]
        y_prompt = encoder_layer(y_prompt, *lw)
        y_sample = encoder_layer(y_sample, *lw)
    return (y_prompt, y_sample)
```

```python
import functools

import jax
import jax.numpy as jnp
from jax import lax
from jax.experimental import pallas as pl
from jax.experimental.pallas import tpu as pltpu

D_MODEL = 2048
D_FF = 5632
D_CONV = 1024
N_GROUPS = 3
HEADS_PER_GROUP = 4
N_HEADS = N_GROUPS * HEADS_PER_GROUP
HEAD_DIM = 128
D_ATTN = HEADS_PER_GROUP * HEAD_DIM
D_QKV = N_GROUPS * D_ATTN
ATTN_WINDOWS = (128, 512, 2048)
ATTN_DILATIONS = (1, 4, 16)
D_IN = 3 * D_CONV + 3 * D_QKV
RMS_EPS = 1e-6
FFN_RESIDUAL_WEIGHT = 0.5

V7X_VMEM_BYTES = 64 * 1024 * 1024
VMEM_LIMIT_BYTES = V7X_VMEM_BYTES - 8 * 1024 * 1024

FFN_ROW_TILE = 512
FFN_FF_TILE = 512
PROJ_ROW_TILE = 1024
PROJ_COL_TILE = 1280
MIX_ROW_TILE = 256
MIX_COL_CHUNK = 512
CONV_COL_TILE = 256
ATTN_Q_TILE = 128

BF16 = jnp.bfloat16
F32 = jnp.float32


def _rms_norm(x, g):
    return x * lax.rsqrt(jnp.mean(x * x, axis=-1, keepdims=True) + RMS_EPS) * g


def _dot(a, b):
    return jnp.dot(a, b, preferred_element_type=F32)


def _sigmoid(x):
    return 1.0 / (1.0 + jnp.exp(-x))


def _ffn_kernel(x_ref, gpre_ref, w1_ref, w3_ref, w2_ref, gpost_ref, y_ref, xn_ref, acc_ref):
    j = pl.program_id(1)

    @pl.when(j == 0)
    def _():
        xn_ref[...] = _rms_norm(x_ref[...], gpre_ref[...]).astype(BF16)

    xn = xn_ref[...]
    a = _dot(xn, w1_ref[...])
    c = _dot(xn, w3_ref[...])
    hidden = (a * _sigmoid(a) * c).astype(BF16)
    part = _dot(hidden, w2_ref[...])

    @pl.when(j == 0)
    def _():
        acc_ref[...] = part

    @pl.when(j > 0)
    def _():
        acc_ref[...] += part

    @pl.when(j == pl.num_programs(1) - 1)
    def _():
        y_ref[...] = x_ref[...] + FFN_RESIDUAL_WEIGHT * _rms_norm(acc_ref[...], gpost_ref[...])


def _ffn(x, g_pre, w1, w3, w2, g_post):
    m = x.shape[0]
    tm, tf = FFN_ROW_TILE, FFN_FF_TILE
    return pl.pallas_call(
        _ffn_kernel,
        grid=(m // tm, D_FF // tf),
        in_specs=[
            pl.BlockSpec((tm, D_MODEL), lambda i, j: (i, 0)),
            pl.BlockSpec((1, D_MODEL), lambda i, j: (0, 0)),
            pl.BlockSpec((D_MODEL, tf), lambda i, j: (0, j)),
            pl.BlockSpec((D_MODEL, tf), lambda i, j: (0, j)),
            pl.BlockSpec((tf, D_MODEL), lambda i, j: (j, 0)),
            pl.BlockSpec((1, D_MODEL), lambda i, j: (0, 0)),
        ],
        out_specs=pl.BlockSpec((tm, D_MODEL), lambda i, j: (i, 0)),
        out_shape=jax.ShapeDtypeStruct((m, D_MODEL), F32),
        scratch_shapes=[pltpu.VMEM((tm, D_MODEL), BF16), pltpu.VMEM((tm, D_MODEL), F32)],
        compiler_params=pltpu.CompilerParams(
            dimension_semantics=("parallel", "arbitrary"), vmem_limit_bytes=VMEM_LIMIT_BYTES),
        name="ffn",
    )(x, g_pre, w1, w3, w2, g_post)


def _in_proj_kernel(h_ref, g_ref, w_ref, o_ref, u_ref):
    @pl.when(pl.program_id(1) == 0)
    def _():
        u_ref[...] = _rms_norm(h_ref[...], g_ref[...]).astype(BF16)

    o_ref[...] = _dot(u_ref[...], w_ref[...]).astype(BF16)


def _in_proj(h, g_mix, w_in):
    m = h.shape[0]
    tm, tn = PROJ_ROW_TILE, PROJ_COL_TILE
    return pl.pallas_call(
        _in_proj_kernel,
        grid=(m // tm, D_IN // tn),
        in_specs=[
            pl.BlockSpec((tm, D_MODEL), lambda i, j: (i, 0)),
            pl.BlockSpec((1, D_MODEL), lambda i, j: (0, 0)),
            pl.BlockSpec((D_MODEL, tn), lambda i, j: (0, j)),
        ],
        out_specs=pl.BlockSpec((tm, tn), lambda i, j: (i, j)),
        out_shape=jax.ShapeDtypeStruct((m, D_IN), BF16),
        scratch_shapes=[pltpu.VMEM((tm, D_MODEL), BF16)],
        compiler_params=pltpu.CompilerParams(
            dimension_semantics=("parallel", "arbitrary"), vmem_limit_bytes=VMEM_LIMIT_BYTES),
        name="in_proj",
    )(h, g_mix, w_in)


def _conv_kernel(cb_ref, cc_ref, cx_ref, w_ref, b_ref, o_ref):
    s = cc_ref.shape[0]
    z = cc_ref[...].astype(F32) * cx_ref[...].astype(F32)
    t = lax.broadcasted_iota(jnp.int32, z.shape, 0)
    z_prev = jnp.where(t == 0, 0.0, pltpu.roll(z, 1, 0))
    z_next = jnp.where(t == s - 1, 0.0, pltpu.roll(z, s - 1, 0))
    w = w_ref[...]
    conv = z_prev * w[0:1] + z * w[1:2] + z_next * w[2:3] + b_ref[...]
    o_ref[...] = (cb_ref[...].astype(F32) * conv).astype(BF16)


def _conv(proj, conv_w, conv_b):
    b, s, _ = proj.shape
    tc = CONV_COL_TILE
    nc = D_CONV // tc
    return pl.pallas_call(
        _conv_kernel,
        grid=(b, nc),
        in_specs=[
            pl.BlockSpec((None, s, tc), lambda i, j: (i, 0, j)),
            pl.BlockSpec((None, s, tc), lambda i, j: (i, 0, nc + j)),
            pl.BlockSpec((None, s, tc), lambda i, j: (i, 0, 2 * nc + j)),
            pl.BlockSpec((3, tc), lambda i, j: (0, j)),
            pl.BlockSpec((1, tc), lambda i, j: (0, j)),
        ],
        out_specs=pl.BlockSpec((None, s, tc), lambda i, j: (i, 0, j)),
        out_shape=jax.ShapeDtypeStruct((b, s, D_CONV), BF16),
        compiler_params=pltpu.CompilerParams(
            dimension_semantics=("parallel", "parallel"), vmem_limit_bytes=VMEM_LIMIT_BYTES),
        name="conv",
    )(proj, proj, proj, conv_w, conv_b)


def _attn_kernel(q_ref, k_ref, v_ref, o_ref, lse_ref, *, group):
    length = q_ref.shape[0]
    dilation = ATTN_DILATIONS[group]
    radius = ATTN_WINDOWS[group] // (2 * dilation)
    tq = ATTN_Q_TILE
    tk = min(tq + 2 * radius, length)
    scale = HEAD_DIM ** -0.5

    def q_block(i, carry):
        q0 = pl.multiple_of(i * tq, tq)
        k0 = pl.multiple_of(jnp.clip(q0 - radius, 0, length - tk), radius)
        q_pos = q0 + lax.broadcasted_iota(jnp.int32, (tq, tk), 0)
        k_pos = k0 + lax.broadcasted_iota(jnp.int32, (tq, tk), 1)
        delta = jnp.abs(k_pos - q_pos)
        valid = delta <= radius
        dist = (dilation * delta).astype(F32)
        for h in range(HEADS_PER_GROUP):
            slope = 2.0 ** (-8.0 * (group * HEADS_PER_GROUP + h + 1) / N_HEADS)
            cols = pl.ds(h * HEAD_DIM, HEAD_DIM)
            q = q_ref[pl.ds(q0, tq), cols]
            k = k_ref[pl.ds(k0, tk), cols]
            v = v_ref[pl.ds(k0, tk), cols]
            s = lax.dot_general(q, k, (((1,), (1,)), ((), ())), preferred_element_type=F32)
            s = jnp.where(valid, s * scale - slope * dist, -jnp.inf)
            m = jnp.max(s, axis=-1, keepdims=True)
            p = jnp.exp(s - m)
            l = jnp.sum(p, axis=-1, keepdims=True)
            o = _dot(p.astype(BF16), v) / l
            o_ref[pl.ds(q0, tq), cols] = o.astype(BF16)
            lse_ref[pl.ds(q0, tq), cols] = jnp.broadcast_to(m + jnp.log(l), (tq, HEAD_DIM))
        return carry

    lax.fori_loop(0, length // tq, q_block, 0)


def _attn_group(proj, group):
    b, s, _ = proj.shape
    d = ATTN_DILATIONS[group]
    length = s // d
    blocks_per_row = D_IN // D_ATTN
    q_blk = 3 * D_CONV // D_ATTN + group
    k_blk = q_blk + D_QKV // D_ATTN
    v_blk = k_blk + D_QKV // D_ATTN
    view = proj.reshape(b, length, d * D_IN)

    def in_spec(blk):
        return pl.BlockSpec((None, length, D_ATTN), lambda i, r: (i, 0, r * blocks_per_row + blk))

    out_spec = pl.BlockSpec((None, length, D_ATTN), lambda i, r: (i, 0, r))
    o, lse = pl.pallas_call(
        functools.partial(_attn_kernel, group=group),
        grid=(b, d),
        in_specs=[in_spec(q_blk), in_spec(k_blk), in_spec(v_blk)],
        out_specs=[out_spec, out_spec],
        out_shape=[jax.ShapeDtypeStruct((b, length, d * D_ATTN), BF16),
                   jax.ShapeDtypeStruct((b, length, d * D_ATTN), F32)],
        compiler_params=pltpu.CompilerParams(
            dimension_semantics=("parallel", "parallel"), vmem_limit_bytes=VMEM_LIMIT_BYTES),
        name=f"attn_g{group}",
    )(view, view, view)
    return o.reshape(b, s, D_ATTN), lse.reshape(b, s, D_ATTN)


def _mix_out_kernel(h_ref, yc_ref, o0_ref, o1_ref, o2_ref, l0_ref, l1_ref, l2_ref,
                    gmix_ref, wco_ref, wao_ref, wg_ref, bg_ref, wo_ref, gpost_ref,
                    out_ref, merged_ref):
    h = h_ref[...]
    u = _rms_norm(h, gmix_ref[...]).astype(BF16)

    l0, l1, l2 = l0_ref[...], l1_ref[...], l2_ref[...]
    mx = jnp.maximum(jnp.maximum(l0, l1), l2)
    e0, e1, e2 = jnp.exp(l0 - mx), jnp.exp(l1 - mx), jnp.exp(l2 - mx)
    den = e0 + e1 + e2
    o = (e0 * o0_ref[...].astype(F32) + e1 * o1_ref[...].astype(F32) + e2 * o2_ref[...].astype(F32)) / den
    o = o.astype(BF16)
    yc = yc_ref[...]

    tn = MIX_COL_CHUNK
    for c in range(D_MODEL // tn):
        lo = c * tn
        g_conv = _sigmoid(_dot(u, wg_ref[:, lo:lo + tn]) + bg_ref[:, lo:lo + tn])
        g_attn = _sigmoid(_dot(u, wg_ref[:, D_MODEL + lo:D_MODEL + lo + tn])
                          + bg_ref[:, D_MODEL + lo:D_MODEL + lo + tn])
        y_conv = _dot(yc, wco_ref[:, lo:lo + tn])
        y_attn = _dot(o, wao_ref[:, lo:lo + tn])
        merged_ref[:, lo:lo + tn] = (g_conv * y_conv + g_attn * y_attn).astype(BF16)

    mix = _dot(merged_ref[...], wo_ref[...])
    out_ref[...] = h + _rms_norm(mix, gpost_ref[...])


def _mix_out(h, yc, o_groups, lse_groups, g_mix, w_conv_out, w_attn_out, w_gate, b_gate, w_o, g_post):
    m = h.shape[0]
    tm = MIX_ROW_TILE

    def rows(width):
        return pl.BlockSpec((tm, width), lambda i: (i, 0))

    def resident(shape):
        return pl.BlockSpec(shape, lambda i: (0, 0), pipeline_mode=pl.Buffered(1))

    return pl.pallas_call(
        _mix_out_kernel,
        grid=(m // tm,),
        in_specs=[
            rows(D_MODEL), rows(D_CONV),
            rows(D_ATTN), rows(D_ATTN), rows(D_ATTN),
            rows(D_ATTN), rows(D_ATTN), rows(D_ATTN),
            resident((1, D_MODEL)),
            resident((D_CONV, D_MODEL)),
            resident((D_ATTN, D_MODEL)),
            resident((D_MODEL, 2 * D_MODEL)),
            resident((1, 2 * D_MODEL)),
            resident((D_MODEL, D_MODEL)),
            resident((1, D_MODEL)),
        ],
        out_specs=rows(D_MODEL),
        out_shape=jax.ShapeDtypeStruct((m, D_MODEL), F32),
        scratch_shapes=[pltpu.VMEM((tm, D_MODEL), BF16)],
        compiler_params=pltpu.CompilerParams(
            dimension_semantics=("parallel",), vmem_limit_bytes=VMEM_LIMIT_BYTES),
        name="mix_out",
    )(h, yc, *o_groups, *lse_groups, g_mix, w_conv_out, w_attn_out, w_gate, b_gate, w_o, g_post)


def _encoder_layer(x, p):
    b, s, _ = x.shape
    m = b * s
    h = _ffn(x.reshape(m, D_MODEL), p["ffn1_norm_pre"], p["ffn1_w1"], p["ffn1_w3"], p["ffn1_w2"],
             p["ffn1_norm_post"])
    proj = _in_proj(h, p["mix_norm_pre"], p["w_in"]).reshape(b, s, D_IN)
    yc = _conv(proj, p["conv_w"], p["conv_b"]).reshape(m, D_CONV)
    o_groups, lse_groups = [], []
    for g in range(N_GROUPS):
        o_g, lse_g = _attn_group(proj, g)
        o_groups.append(o_g.reshape(m, D_ATTN))
        lse_groups.append(lse_g.reshape(m, D_ATTN))
    h = _mix_out(h, yc, o_groups, lse_groups, p["mix_norm_pre"], p["w_conv_out"], p["w_attn_out"],
                 p["w_gate"], p["b_gate"], p["w_o"], p["mix_norm_post"])
    y = _ffn(h, p["ffn2_norm_pre"], p["ffn2_w1"], p["ffn2_w3"], p["ffn2_w2"], p["ffn2_norm_post"])
    return y.reshape(b, s, D_MODEL)


_MATMUL_WEIGHTS = ("ffn1_w1", "ffn1_w3", "ffn1_w2", "w_in", "w_conv_out", "w_attn_out", "w_gate", "w_o",
                   "ffn2_w1", "ffn2_w3", "ffn2_w2")
_ROW_VECTORS = ("ffn1_norm_pre", "ffn1_norm_post", "mix_norm_pre", "conv_b", "b_gate", "mix_norm_post",
                "ffn2_norm_pre", "ffn2_norm_post")


def kernel(x_prompt, x_sample, ffn1_norm_pre, ffn1_w1, ffn1_w3, ffn1_w2, ffn1_norm_post, mix_norm_pre, w_in, conv_w, conv_b, w_conv_out, w_attn_out, w_gate, b_gate, w_o, mix_norm_post, ffn2_norm_pre, ffn2_w1, ffn2_w3, ffn2_w2, ffn2_norm_post):
    stacked = dict(
        ffn1_norm_pre=ffn1_norm_pre, ffn1_w1=ffn1_w1, ffn1_w3=ffn1_w3, ffn1_w2=ffn1_w2,
        ffn1_norm_post=ffn1_norm_post, mix_norm_pre=mix_norm_pre, w_in=w_in, conv_w=conv_w, conv_b=conv_b,
        w_conv_out=w_conv_out, w_attn_out=w_attn_out, w_gate=w_gate, b_gate=b_gate, w_o=w_o,
        mix_norm_post=mix_norm_post, ffn2_norm_pre=ffn2_norm_pre, ffn2_w1=ffn2_w1, ffn2_w3=ffn2_w3,
        ffn2_w2=ffn2_w2, ffn2_norm_post=ffn2_norm_post)
    depth = w_in.shape[0]
    y_prompt, y_sample = x_prompt, x_sample
    for layer in range(depth):
        p = {}
        for name, w in stacked.items():
            w = w[layer]
            if name in _MATMUL_WEIGHTS:
                w = w.astype(BF16)
            elif name in _ROW_VECTORS:
                w = w.reshape(1, -1)
            p[name] = w
        y_prompt = _encoder_layer(y_prompt, p)
        y_sample = _encoder_layer(y_sample, p)
    return (y_prompt, y_sample)
```

```python
import functools

import jax
import jax.numpy as jnp
from jax import lax
from jax.experimental import pallas as pl
from jax.experimental.pallas import tpu as pltpu

D_MODEL = 2048
D_FF = 5632
D_CONV = 1024
N_GROUPS = 3
HEADS_PER_GROUP = 4
N_HEADS = N_GROUPS * HEADS_PER_GROUP
HEAD_DIM = 128
D_ATTN = HEADS_PER_GROUP * HEAD_DIM
D_QKV = N_GROUPS * D_ATTN
ATTN_WINDOWS = (128, 512, 2048)
ATTN_DILATIONS = (1, 4, 16)
D_IN = 3 * D_CONV + 3 * D_QKV
D_NATURAL = 3 * D_CONV + 3 * D_ATTN
RMS_EPS = 1e-6
FFN_RESIDUAL_WEIGHT = 0.5

V7X_VMEM_BYTES = 64 * 1024 * 1024
VMEM_LIMIT_BYTES = V7X_VMEM_BYTES - 8 * 1024 * 1024

FFN_ROW_TILE = 512
FFN_FF_TILE = 512
PROJ_ROW_TILE = 1024
PROJ_COL_TILE = 1536
MIX_ROW_TILE = 256
MIX_COL_CHUNK = 512
CONV_COL_TILE = 256
ATTN_Q_TILE = 128
ATTN_LOOKAHEAD = 2

LANES = 128
MXU_COLS = 256

BF16 = jnp.bfloat16
F32 = jnp.float32


def _rms_norm(x, g):
    return x * lax.rsqrt(jnp.mean(x * x, axis=-1, keepdims=True) + RMS_EPS) * g


def _dot(a, b):
    return jnp.dot(a, b, preferred_element_type=F32)


def _sigmoid(x):
    return 1.0 / (1.0 + jnp.exp(-x))


def _params(*semantics):
    return pltpu.CompilerParams(dimension_semantics=semantics, vmem_limit_bytes=VMEM_LIMIT_BYTES)


def _ffn_kernel(*refs, emit_normed):
    if emit_normed:
        x_ref, gpre_ref, w1_ref, w3_ref, w2_ref, gpost_ref, gnext_ref, y_ref, u_ref, xn_ref = refs
    else:
        x_ref, gpre_ref, w1_ref, w3_ref, w2_ref, gpost_ref, y_ref, xn_ref = refs
    j = pl.program_id(1)

    @pl.when(j == 0)
    def _():
        xn_ref[...] = _rms_norm(x_ref[...], gpre_ref[...]).astype(BF16)
        y_ref[...] = jnp.zeros_like(y_ref)

    xn = xn_ref[...]
    a = _dot(xn, w1_ref[...])
    c = _dot(xn, w3_ref[...])
    hidden = (a * _sigmoid(a) * c).astype(BF16)
    y_ref[...] += _dot(hidden, w2_ref[...])

    @pl.when(j == pl.num_programs(1) - 1)
    def _():
        y = x_ref[...] + FFN_RESIDUAL_WEIGHT * _rms_norm(y_ref[...], gpost_ref[...])
        y_ref[...] = y
        if emit_normed:
            u_ref[...] = _rms_norm(y, gnext_ref[...]).astype(BF16)


def _ffn(x, g_pre, w1, w3, w2, g_post, g_next=None):
    m = x.shape[0]
    tm, tf = FFN_ROW_TILE, FFN_FF_TILE
    emit_normed = g_next is not None
    row_spec = pl.BlockSpec((tm, D_MODEL), lambda i, j: (i, 0))
    gain_spec = pl.BlockSpec((1, D_MODEL), lambda i, j: (0, 0))
    in_specs = [row_spec, gain_spec,
                pl.BlockSpec((D_MODEL, tf), lambda i, j: (0, j)),
                pl.BlockSpec((D_MODEL, tf), lambda i, j: (0, j)),
                pl.BlockSpec((tf, D_MODEL), lambda i, j: (j, 0)),
                gain_spec]
    args = [x, g_pre, w1, w3, w2, g_post]
    out_specs = [row_spec]
    out_shape = [jax.ShapeDtypeStruct((m, D_MODEL), F32)]
    if emit_normed:
        in_specs.append(gain_spec)
        args.append(g_next)
        out_specs.append(row_spec)
        out_shape.append(jax.ShapeDtypeStruct((m, D_MODEL), BF16))
    outs = pl.pallas_call(
        functools.partial(_ffn_kernel, emit_normed=emit_normed),
        grid=(m // tm, D_FF // tf),
        in_specs=in_specs,
        out_specs=out_specs,
        out_shape=out_shape,
        scratch_shapes=[pltpu.VMEM((tm, D_MODEL), BF16)],
        compiler_params=_params("parallel", "arbitrary"),
        name="ffn",
    )(*args)
    return outs if emit_normed else outs[0]


def _proj_kernel(u_ref, w_ref, o_ref, *scratch, dilation):
    if dilation == 1:
        o_ref[0] = _dot(u_ref[...], w_ref[...]).astype(BF16)
        return
    (res_ref,) = scratch
    u = u_ref[...]
    n_slabs, tm, _ = res_ref.shape
    for c in range(0, n_slabs, MXU_COLS // LANES):
        res = _dot(u, w_ref[:, c * LANES:c * LANES + MXU_COLS])
        for k in range(MXU_COLS // LANES):
            res_ref[c + k] = res[:, k * LANES:(k + 1) * LANES]
    rows = tm // dilation
    for r in range(dilation):
        for c in range(n_slabs):
            o_ref[r, :, c * LANES:(c + 1) * LANES] = res_ref[c, pl.ds(r, rows, stride=dilation), :].astype(BF16)


def _proj(u, w, dilation):
    b, s, _ = u.shape
    n = w.shape[1]
    tm, tn = PROJ_ROW_TILE, PROJ_COL_TILE
    scratch = [] if dilation == 1 else [pltpu.VMEM((tn // LANES, tm, LANES), F32)]
    return pl.pallas_call(
        functools.partial(_proj_kernel, dilation=dilation),
        grid=(b, s // tm, n // tn),
        in_specs=[
            pl.BlockSpec((None, tm, D_MODEL), lambda bi, i, j: (bi, i, 0)),
            pl.BlockSpec((D_MODEL, tn), lambda bi, i, j: (0, j)),
        ],
        out_specs=pl.BlockSpec((None, dilation, tm // dilation, tn), lambda bi, i, j: (bi, 0, i, j)),
        out_shape=jax.ShapeDtypeStruct((b, dilation, s // dilation, n), BF16),
        scratch_shapes=scratch,
        compiler_params=_params("parallel", "parallel", "arbitrary"),
        name=f"proj_d{dilation}",
    )(u, w)


def _conv_kernel(cb_ref, cc_ref, cx_ref, w_ref, b_ref, o_ref):
    s = cc_ref.shape[0]
    z = cc_ref[...].astype(F32) * cx_ref[...].astype(F32)
    t = lax.broadcasted_iota(jnp.int32, z.shape, 0)
    z_prev = jnp.where(t == 0, 0.0, pltpu.roll(z, 1, 0))
    z_next = jnp.where(t == s - 1, 0.0, pltpu.roll(z, s - 1, 0))
    w = w_ref[...]
    conv = z_prev * w[0:1] + z * w[1:2] + z_next * w[2:3] + b_ref[...]
    o_ref[...] = (cb_ref[...].astype(F32) * conv).astype(BF16)


def _conv(proj, conv_w, conv_b):
    b, s, _ = proj.shape
    tc = CONV_COL_TILE
    nc = D_CONV // tc
    return pl.pallas_call(
        _conv_kernel,
        grid=(b, nc),
        in_specs=[
            pl.BlockSpec((None, s, tc), lambda i, j: (i, 0, j)),
            pl.BlockSpec((None, s, tc), lambda i, j: (i, 0, nc + j)),
            pl.BlockSpec((None, s, tc), lambda i, j: (i, 0, 2 * nc + j)),
            pl.BlockSpec((3, tc), lambda i, j: (0, j)),
            pl.BlockSpec((1, tc), lambda i, j: (0, j)),
        ],
        out_specs=pl.BlockSpec((None, s, tc), lambda i, j: (i, 0, j)),
        out_shape=jax.ShapeDtypeStruct((b, s, D_CONV), BF16),
        compiler_params=_params("parallel", "parallel"),
        name="conv",
    )(proj, proj, proj, conv_w, conv_b)


def _attn_geometry(group, length):
    dilation = ATTN_DILATIONS[group]
    radius = ATTN_WINDOWS[group] // (2 * dilation)
    tq = ATTN_Q_TILE
    tk = min(tq + 2 * radius, length)
    offsets = (0,) if tk == length else (0, radius, tk - tq)
    return dilation, radius, tq, tk, offsets


def _attn_kernel(q_ref, k_ref, v_ref, o_ref, lse_ref, bias_ref, *scratch, group):
    length = q_ref.shape[1]
    dilation, radius, tq, tk, offsets = _attn_geometry(group, length)
    scale = HEAD_DIM ** -0.5
    o_nat_ref = o_ref if dilation == 1 else scratch[0]

    @pl.when(pl.program_id(0) == 0)
    def _():
        row = lax.broadcasted_iota(jnp.int32, (tq, tk), 0)
        col = lax.broadcasted_iota(jnp.int32, (tq, tk), 1)
        for vi, off in enumerate(offsets):
            delta = jnp.abs(col - row - off)
            dist = (dilation * delta).astype(F32)
            for h in range(HEADS_PER_GROUP):
                slope = 2.0 ** (-8.0 * (group * HEADS_PER_GROUP + h + 1) / N_HEADS)
                bias_ref[vi * HEADS_PER_GROUP + h] = jnp.where(delta <= radius, -slope * dist, -jnp.inf)

    def key_start(q0):
        return min(max(q0 - radius, 0), length - tk)

    def scores(r, q0):
        k0 = key_start(q0)
        out = []
        for h in range(HEADS_PER_GROUP):
            cols = pl.ds(h * HEAD_DIM, HEAD_DIM)
            q = q_ref[r, pl.ds(q0, tq), cols]
            k = k_ref[r, pl.ds(k0, tk), cols]
            out.append(lax.dot_general(q, k, (((1,), (1,)), ((), ())), preferred_element_type=F32))
        return out

    def softmax_pv(r, q0, s_heads):
        k0 = key_start(q0)
        vi = offsets.index(q0 - k0)
        out_rows = pl.ds(q0, tq) if dilation == 1 else pl.ds(q0 * dilation + r, tq, stride=dilation)
        for h, s in enumerate(s_heads):
            v = v_ref[r, pl.ds(k0, tk), pl.ds(h * HEAD_DIM, HEAD_DIM)]
            s = s * scale + bias_ref[vi * HEADS_PER_GROUP + h]
            m = jnp.max(s, axis=-1, keepdims=True)
            p = jnp.exp(s - m)
            l = jnp.sum(p, axis=-1, keepdims=True)
            o = _dot(p.astype(BF16), v) / l
            o_nat_ref[h, out_rows, :] = o.astype(o_nat_ref.dtype)
            lse_ref[h, out_rows, :] = jnp.broadcast_to(m + jnp.log(l), (tq, HEAD_DIM))

    tiles = [(r, it * tq) for r in range(dilation) for it in range(length // tq)]
    in_flight = []
    for idx in range(len(tiles) + ATTN_LOOKAHEAD):
        if idx < len(tiles):
            in_flight.append(scores(*tiles[idx]))
        if idx >= ATTN_LOOKAHEAD:
            softmax_pv(*tiles[idx - ATTN_LOOKAHEAD], in_flight.pop(0))

    if dilation != 1:
        o_ref[...] = o_nat_ref[...].astype(BF16)


def _attn_group(qkv, group, first_col_block):
    b, d, length, _ = qkv.shape
    s = d * length
    _, _, tq, tk, offsets = _attn_geometry(group, length)
    head_major = (HEADS_PER_GROUP, s, HEAD_DIM)

    def in_spec(blk):
        return pl.BlockSpec((None, d, length, D_ATTN), lambda i: (i, 0, 0, first_col_block + blk))

    out_spec = pl.BlockSpec((None,) + head_major, lambda i: (i, 0, 0, 0))
    scratch = [pltpu.VMEM((len(offsets) * HEADS_PER_GROUP, tq, tk), F32)]
    if d != 1:
        scratch.append(pltpu.VMEM(head_major, F32))
    return pl.pallas_call(
        functools.partial(_attn_kernel, group=group),
        grid=(b,),
        in_specs=[in_spec(0), in_spec(1), in_spec(2)],
        out_specs=[out_spec, out_spec],
        out_shape=[jax.ShapeDtypeStruct((b,) + head_major, BF16), jax.ShapeDtypeStruct((b,) + head_major, F32)],
        scratch_shapes=scratch,
        compiler_params=_params("arbitrary"),
        name=f"attn_g{group}",
    )(qkv, qkv, qkv)


def _mix_out_kernel(h_ref, u_ref, yc_ref, o0_ref, o1_ref, o2_ref, l0_ref, l1_ref, l2_ref,
                    wco_ref, wao_ref, wg_ref, bg_ref, wo_ref, gpost_ref, out_ref, merged_ref):
    l0, l1, l2 = l0_ref[...], l1_ref[...], l2_ref[...]
    mx = jnp.maximum(jnp.maximum(l0, l1), l2)
    e0, e1, e2 = jnp.exp(l0 - mx), jnp.exp(l1 - mx), jnp.exp(l2 - mx)
    den = e0 + e1 + e2
    o = (e0 * o0_ref[...].astype(F32) + e1 * o1_ref[...].astype(F32) + e2 * o2_ref[...].astype(F32)) / den
    o = o.astype(BF16)
    o = jnp.concatenate([o[hd] for hd in range(HEADS_PER_GROUP)], axis=-1)
    yc = yc_ref[...]
    u = u_ref[...]

    tn = MIX_COL_CHUNK
    for c in range(D_MODEL // tn):
        lo = c * tn
        g_conv = _sigmoid(_dot(u, wg_ref[:, lo:lo + tn]) + bg_ref[:, lo:lo + tn])
        g_attn = _sigmoid(_dot(u, wg_ref[:, D_MODEL + lo:D_MODEL + lo + tn])
                          + bg_ref[:, D_MODEL + lo:D_MODEL + lo + tn])
        y_conv = _dot(yc, wco_ref[:, lo:lo + tn])
        y_attn = _dot(o, wao_ref[:, lo:lo + tn])
        merged_ref[:, lo:lo + tn] = (g_conv * y_conv + g_attn * y_attn).astype(BF16)

    mix = _dot(merged_ref[...], wo_ref[...])
    out_ref[...] = h_ref[...] + _rms_norm(mix, gpost_ref[...])


def _mix_out(h, u, yc, o_groups, lse_groups, w_conv_out, w_attn_out, w_gate, b_gate, w_o, g_post):
    b, s, _ = h.shape
    tm = MIX_ROW_TILE

    def rows(width):
        return pl.BlockSpec((None, tm, width), lambda bi, i: (bi, i, 0))

    heads = pl.BlockSpec((None, HEADS_PER_GROUP, tm, HEAD_DIM), lambda bi, i: (bi, 0, i, 0))

    def resident(shape):
        return pl.BlockSpec(shape, lambda bi, i: (0, 0), pipeline_mode=pl.Buffered(1))

    return pl.pallas_call(
        _mix_out_kernel,
        grid=(b, s // tm),
        in_specs=[
            rows(D_MODEL), rows(D_MODEL), rows(D_CONV),
            heads, heads, heads,
            heads, heads, heads,
            resident((D_CONV, D_MODEL)),
            resident((D_ATTN, D_MODEL)),
            resident((D_MODEL, 2 * D_MODEL)),
            resident((1, 2 * D_MODEL)),
            resident((D_MODEL, D_MODEL)),
            resident((1, D_MODEL)),
        ],
        out_specs=rows(D_MODEL),
        out_shape=jax.ShapeDtypeStruct((b, s, D_MODEL), F32),
        scratch_shapes=[pltpu.VMEM((tm, D_MODEL), BF16)],
        compiler_params=_params("parallel", "parallel"),
        name="mix_out",
    )(h, u, yc, *o_groups, *lse_groups, w_conv_out, w_attn_out, w_gate, b_gate, w_o, g_post)


def _encoder_layer(x, p):
    b, s, _ = x.shape
    m = b * s
    h, u = _ffn(x.reshape(m, D_MODEL), p["ffn1_norm_pre"], p["ffn1_w1"], p["ffn1_w3"], p["ffn1_w2"],
                p["ffn1_norm_post"], p["mix_norm_pre"])
    u = u.reshape(b, s, D_MODEL)
    natural = _proj(u, p["w_in_natural"], 1)
    yc = _conv(natural.reshape(b, s, D_NATURAL), p["conv_w"], p["conv_b"])
    o_groups, lse_groups = [], []
    for g in range(N_GROUPS):
        if g == 0:
            o_g, lse_g = _attn_group(natural, g, 3 * D_CONV // D_ATTN)
        else:
            o_g, lse_g = _attn_group(_proj(u, p["w_in_groups"][g], ATTN_DILATIONS[g]), g, 0)
        o_groups.append(o_g)
        lse_groups.append(lse_g)
    h = _mix_out(h.reshape(b, s, D_MODEL), u, yc, o_groups, lse_groups, p["w_conv_out"], p["w_attn_out"],
                 p["w_gate"], p["b_gate"], p["w_o"], p["mix_norm_post"])
    y = _ffn(h.reshape(m, D_MODEL), p["ffn2_norm_pre"], p["ffn2_w1"], p["ffn2_w3"], p["ffn2_w2"],
             p["ffn2_norm_post"])
    return y.reshape(b, s, D_MODEL)


_MATMUL_WEIGHTS = ("ffn1_w1", "ffn1_w3", "ffn1_w2", "w_in", "w_conv_out", "w_attn_out", "w_gate", "w_o",
                   "ffn2_w1", "ffn2_w3", "ffn2_w2")
_ROW_VECTORS = ("ffn1_norm_pre", "ffn1_norm_post", "mix_norm_pre", "conv_b", "b_gate", "mix_norm_post",
                "ffn2_norm_pre", "ffn2_norm_post")


def _group_qkv_columns(w_in, group):
    base = 3 * D_CONV + group * D_ATTN
    return [w_in[:, base + i * D_QKV: base + i * D_QKV + D_ATTN] for i in range(3)]


def kernel(x_prompt, x_sample, ffn1_norm_pre, ffn1_w1, ffn1_w3, ffn1_w2, ffn1_norm_post, mix_norm_pre, w_in, conv_w, conv_b, w_conv_out, w_attn_out, w_gate, b_gate, w_o, mix_norm_post, ffn2_norm_pre, ffn2_w1, ffn2_w3, ffn2_w2, ffn2_norm_post):
    stacked = dict(
        ffn1_norm_pre=ffn1_norm_pre, ffn1_w1=ffn1_w1, ffn1_w3=ffn1_w3, ffn1_w2=ffn1_w2,
        ffn1_norm_post=ffn1_norm_post, mix_norm_pre=mix_norm_pre, w_in=w_in, conv_w=conv_w, conv_b=conv_b,
        w_conv_out=w_conv_out, w_attn_out=w_attn_out, w_gate=w_gate, b_gate=b_gate, w_o=w_o,
        mix_norm_post=mix_norm_post, ffn2_norm_pre=ffn2_norm_pre, ffn2_w1=ffn2_w1, ffn2_w3=ffn2_w3,
        ffn2_w2=ffn2_w2, ffn2_norm_post=ffn2_norm_post)
    depth = w_in.shape[0]
    y_prompt, y_sample = x_prompt, x_sample
    for layer in range(depth):
        p = {}
        for name, w in stacked.items():
            w = w[layer]
            if name in _MATMUL_WEIGHTS:
                w = w.astype(BF16)
            elif name in _ROW_VECTORS:
                w = w.reshape(1, -1)
            p[name] = w
        w_in_l = p.pop("w_in")
        p["w_in_natural"] = jnp.concatenate([w_in_l[:, :3 * D_CONV]] + _group_qkv_columns(w_in_l, 0), axis=1)
        p["w_in_groups"] = {g: jnp.concatenate(_group_qkv_columns(w_in_l, g), axis=1) for g in range(1, N_GROUPS)}
        y_prompt = _encoder_layer(y_prompt, p)
        y_sample = _encoder_layer(y_sample, p)
    return (y_prompt, y_sample)
```

```python
import functools

import jax
import jax.numpy as jnp
from jax import lax
from jax.experimental import pallas as pl
from jax.experimental.pallas import tpu as pltpu

D_MODEL = 2048
D_FF = 5632
D_CONV = 1024
N_GROUPS = 3
HEADS_PER_GROUP = 4
N_HEADS = N_GROUPS * HEADS_PER_GROUP
HEAD_DIM = 128
D_ATTN = HEADS_PER_GROUP * HEAD_DIM
D_QKV = N_GROUPS * D_ATTN
ATTN_WINDOWS = (128, 512, 2048)
ATTN_DILATIONS = (1, 4, 16)
D_IN = 3 * D_CONV + 3 * D_QKV
D_NATURAL = 3 * D_CONV + 3 * D_ATTN
RMS_EPS = 1e-6
FFN_RESIDUAL_WEIGHT = 0.5
LOG2_E = 1.4426950408889634
LN_2 = 0.6931471805599453

V7X_VMEM_BYTES = 64 * 1024 * 1024
VMEM_LIMIT_BYTES = V7X_VMEM_BYTES - 8 * 1024 * 1024

FFN_ROW_TILE = 512
FFN_FF_TILE = 512
PROJ_ROW_TILE = 1024
PROJ_COL_TILE = 1536
MIX_ROW_TILE = 256
MIX_COL_CHUNK = 512
CONV_COL_TILE = 256
ATTN_Q_TILE = 128
ATTN_LOOKAHEAD = 2

LANES = 128
MXU_COLS = 256
MAX_SUBLANE_STRIDE = 4

BF16 = jnp.bfloat16
F32 = jnp.float32


def _rms_norm(x, g):
    return x * lax.rsqrt(jnp.mean(x * x, axis=-1, keepdims=True) + RMS_EPS) * g


def _dot(a, b):
    return jnp.dot(a, b, preferred_element_type=F32)


def _sigmoid(x):
    return 1.0 / (1.0 + jnp.exp(-x))


def _params(*semantics):
    return pltpu.CompilerParams(dimension_semantics=semantics, vmem_limit_bytes=VMEM_LIMIT_BYTES)


def _ffn_kernel(*refs, emit_normed):
    if emit_normed:
        x_ref, gpre_ref, w1_ref, w3_ref, w2_ref, gpost_ref, gnext_ref, y_ref, u_ref = refs
    else:
        x_ref, gpre_ref, w1_ref, w3_ref, w2_ref, gpost_ref, y_ref = refs
    j = pl.program_id(1)

    @pl.when(j == 0)
    def _():
        y_ref[...] = jnp.zeros_like(y_ref)

    x = x_ref[...]
    rinv = lax.rsqrt(jnp.mean(x * x, axis=-1, keepdims=True) + RMS_EPS)
    xg = (x * gpre_ref[...]).astype(BF16)
    a = _dot(xg, w1_ref[...]) * rinv
    c = _dot(xg, w3_ref[...]) * rinv
    hidden = (a * _sigmoid(a) * c).astype(BF16)
    y_ref[...] += _dot(hidden, w2_ref[...])

    @pl.when(j == pl.num_programs(1) - 1)
    def _():
        y = x_ref[...] + _rms_norm(y_ref[...], gpost_ref[...])
        y_ref[...] = y
        if emit_normed:
            u_ref[...] = _rms_norm(y, gnext_ref[...]).astype(BF16)


def _ffn(x, g_pre, w1, w3, w2, g_post, g_next=None):
    m = x.shape[0]
    tm, tf = FFN_ROW_TILE, FFN_FF_TILE
    emit_normed = g_next is not None
    row_spec = pl.BlockSpec((tm, D_MODEL), lambda i, j: (i, 0))
    gain_spec = pl.BlockSpec((1, D_MODEL), lambda i, j: (0, 0))
    in_specs = [row_spec, gain_spec,
                pl.BlockSpec((D_MODEL, tf), lambda i, j: (0, j)),
                pl.BlockSpec((D_MODEL, tf), lambda i, j: (0, j)),
                pl.BlockSpec((tf, D_MODEL), lambda i, j: (j, 0)),
                gain_spec]
    args = [x, g_pre, w1, w3, w2, FFN_RESIDUAL_WEIGHT * g_post]
    out_specs = [row_spec]
    out_shape = [jax.ShapeDtypeStruct((m, D_MODEL), F32)]
    if emit_normed:
        in_specs.append(gain_spec)
        args.append(g_next)
        out_specs.append(row_spec)
        out_shape.append(jax.ShapeDtypeStruct((m, D_MODEL), BF16))
    outs = pl.pallas_call(
        functools.partial(_ffn_kernel, emit_normed=emit_normed),
        grid=(m // tm, D_FF // tf),
        in_specs=in_specs,
        out_specs=out_specs,
        out_shape=out_shape,
        compiler_params=_params("parallel", "arbitrary"),
        name="ffn",
    )(*args)
    return outs if emit_normed else outs[0]


def _proj_kernel(u_ref, w_ref, o_ref, *scratch, dilation):
    if dilation == 1:
        o_ref[0] = _dot(u_ref[...], w_ref[...]).astype(BF16)
        return
    slabs = MXU_COLS // LANES
    rows = u_ref.shape[0] // dilation
    n_tiles = w_ref.shape[1] // MXU_COLS
    scratch, staging = scratch[:n_tiles], scratch[n_tiles:]

    def multiply(c):
        res = _dot(u_ref[...], w_ref[:, c * MXU_COLS:(c + 1) * MXU_COLS])
        for k in range(slabs):
            scratch[c][k] = res[:, k * LANES:(k + 1) * LANES]

    def regroup(c):
        for k in range(slabs):
            lo = c * MXU_COLS + k * LANES
            if dilation <= MAX_SUBLANE_STRIDE:
                for r in range(dilation):
                    o_ref[r, :, lo:lo + LANES] = scratch[c][k, pl.ds(r, rows, stride=dilation), :].astype(BF16)
                continue
            inner, outer = MAX_SUBLANE_STRIDE, dilation // MAX_SUBLANE_STRIDE
            tmp_ref = staging[c * slabs + k]
            for r0 in range(inner):
                tmp_ref[r0] = scratch[c][k, pl.ds(r0, rows * outer, stride=inner), :]
            for r0 in range(inner):
                for r1 in range(outer):
                    o_ref[r1 * inner + r0, :, lo:lo + LANES] = (
                        tmp_ref[r0, pl.ds(r1, rows, stride=outer), :].astype(BF16))

    for c in range(n_tiles + 1):
        if c < n_tiles:
            multiply(c)
        if c > 0:
            regroup(c - 1)


def _proj(u, w, dilation):
    b, s, _ = u.shape
    n = w.shape[1]
    tm, tn = PROJ_ROW_TILE, PROJ_COL_TILE
    scratch = []
    if dilation > 1:
        scratch += [pltpu.VMEM((MXU_COLS // LANES, tm, LANES), F32)] * (tn // MXU_COLS)
    if dilation > MAX_SUBLANE_STRIDE:
        scratch += [pltpu.VMEM((MAX_SUBLANE_STRIDE, tm // MAX_SUBLANE_STRIDE, LANES), F32)] * (tn // LANES)
    return pl.pallas_call(
        functools.partial(_proj_kernel, dilation=dilation),
        grid=(b, s // tm, n // tn),
        in_specs=[
            pl.BlockSpec((None, tm, D_MODEL), lambda bi, i, j: (bi, i, 0)),
            pl.BlockSpec((D_MODEL, tn), lambda bi, i, j: (0, j)),
        ],
        out_specs=pl.BlockSpec((None, dilation, tm // dilation, tn), lambda bi, i, j: (bi, 0, i, j)),
        out_shape=jax.ShapeDtypeStruct((b, dilation, s // dilation, n), BF16),
        scratch_shapes=scratch,
        compiler_params=_params("parallel", "parallel", "arbitrary"),
        name=f"proj_d{dilation}",
    )(u, w)


def _conv_kernel(cb_ref, cc_ref, cx_ref, w_ref, b_ref, o_ref):
    s = cc_ref.shape[0]
    z = cc_ref[...].astype(F32) * cx_ref[...].astype(F32)
    t = lax.broadcasted_iota(jnp.int32, z.shape, 0)
    z_prev = jnp.where(t == 0, 0.0, pltpu.roll(z, 1, 0))
    z_next = jnp.where(t == s - 1, 0.0, pltpu.roll(z, s - 1, 0))
    w = w_ref[...]
    conv = z_prev * w[0:1] + z * w[1:2] + z_next * w[2:3] + b_ref[...]
    o_ref[...] = (cb_ref[...].astype(F32) * conv).astype(BF16)


def _conv(proj, conv_w, conv_b):
    b, s, _ = proj.shape
    tc = CONV_COL_TILE
    nc = D_CONV // tc
    return pl.pallas_call(
        _conv_kernel,
        grid=(b, nc),
        in_specs=[
            pl.BlockSpec((None, s, tc), lambda i, j: (i, 0, j)),
            pl.BlockSpec((None, s, tc), lambda i, j: (i, 0, nc + j)),
            pl.BlockSpec((None, s, tc), lambda i, j: (i, 0, 2 * nc + j)),
            pl.BlockSpec((3, tc), lambda i, j: (0, j)),
            pl.BlockSpec((1, tc), lambda i, j: (0, j)),
        ],
        out_specs=pl.BlockSpec((None, s, tc), lambda i, j: (i, 0, j)),
        out_shape=jax.ShapeDtypeStruct((b, s, D_CONV), BF16),
        compiler_params=_params("parallel", "parallel"),
        name="conv",
    )(proj, proj, proj, conv_w, conv_b)


def _attn_geometry(group, length):
    dilation = ATTN_DILATIONS[group]
    radius = ATTN_WINDOWS[group] // (2 * dilation)
    tq = ATTN_Q_TILE
    tk = min(tq + 2 * radius, length)
    offsets = (0,) if tk == length else (0, radius, tk - tq)
    return dilation, radius, tq, tk, offsets


def _attn_kernel(q_ref, k_ref, v_ref, o_ref, lse_ref, bias_ref, *scratch, group):
    length = q_ref.shape[1]
    dilation, radius, tq, tk, offsets = _attn_geometry(group, length)
    scale = HEAD_DIM ** -0.5 * LOG2_E
    o_nat_ref = o_ref if dilation == 1 else scratch[0]

    @pl.when(pl.program_id(0) == 0)
    def _():
        row = lax.broadcasted_iota(jnp.int32, (tq, tk), 0)
        col = lax.broadcasted_iota(jnp.int32, (tq, tk), 1)
        for vi, off in enumerate(offsets):
            delta = jnp.abs(col - row - off)
            dist = (dilation * delta).astype(F32)
            for h in range(HEADS_PER_GROUP):
                slope = 2.0 ** (-8.0 * (group * HEADS_PER_GROUP + h + 1) / N_HEADS)
                bias_ref[vi * HEADS_PER_GROUP + h] = jnp.where(delta <= radius, -slope * LOG2_E * dist, -jnp.inf)

    def key_start(q0):
        return min(max(q0 - radius, 0), length - tk)

    def scores(r, q0):
        k0 = key_start(q0)
        out = []
        for h in range(HEADS_PER_GROUP):
            cols = pl.ds(h * HEAD_DIM, HEAD_DIM)
            q = q_ref[r, pl.ds(q0, tq), cols]
            k = k_ref[r, pl.ds(k0, tk), cols]
            out.append(lax.dot_general(q, k, (((1,), (1,)), ((), ())), preferred_element_type=F32))
        return out

    def softmax_pv(r, q0, s_heads):
        k0 = key_start(q0)
        vi = offsets.index(q0 - k0)
        out_rows = pl.ds(q0, tq) if dilation == 1 else pl.ds(q0 * dilation + r, tq, stride=dilation)
        for h, s in enumerate(s_heads):
            v = v_ref[r, pl.ds(k0, tk), pl.ds(h * HEAD_DIM, HEAD_DIM)]
            s = s * scale + bias_ref[vi * HEADS_PER_GROUP + h]
            m = jnp.max(s, axis=-1, keepdims=True)
            p = jnp.exp2(s - m)
            l = jnp.sum(p, axis=-1, keepdims=True)
            o = _dot(p.astype(BF16), v) / l
            o_nat_ref[h, out_rows, :] = o.astype(o_nat_ref.dtype)
            lse_ref[h, out_rows, :] = jnp.broadcast_to(m * LN_2 + jnp.log(l), (tq, HEAD_DIM))

    tiles = [(r, it * tq) for r in range(dilation) for it in range(length // tq)]
    in_flight = []
    for idx in range(len(tiles) + ATTN_LOOKAHEAD):
        if idx < len(tiles):
            in_flight.append(scores(*tiles[idx]))
        if idx >= ATTN_LOOKAHEAD:
            softmax_pv(*tiles[idx - ATTN_LOOKAHEAD], in_flight.pop(0))

    if dilation != 1:
        o_ref[...] = o_nat_ref[...].astype(BF16)


def _attn_group(qkv, group, first_col_block):
    b, d, length, _ = qkv.shape
    s = d * length
    _, _, tq, tk, offsets = _attn_geometry(group, length)
    head_major = (HEADS_PER_GROUP, s, HEAD_DIM)

    def in_spec(blk):
        return pl.BlockSpec((None, d, length, D_ATTN), lambda i: (i, 0, 0, first_col_block + blk))

    out_spec = pl.BlockSpec((None,) + head_major, lambda i: (i, 0, 0, 0))
    scratch = [pltpu.VMEM((len(offsets) * HEADS_PER_GROUP, tq, tk), F32)]
    if d != 1:
        scratch.append(pltpu.VMEM(head_major, F32))
    return pl.pallas_call(
        functools.partial(_attn_kernel, group=group),
        grid=(b,),
        in_specs=[in_spec(0), in_spec(1), in_spec(2)],
        out_specs=[out_spec, out_spec],
        out_shape=[jax.ShapeDtypeStruct((b,) + head_major, BF16), jax.ShapeDtypeStruct((b,) + head_major, F32)],
        scratch_shapes=scratch,
        compiler_params=_params("arbitrary"),
        name=f"attn_g{group}",
    )(qkv, qkv, qkv)


def _mix_out_kernel(h_ref, u_ref, yc_ref, o0_ref, o1_ref, o2_ref, l0_ref, l1_ref, l2_ref,
                    wco_ref, wao_ref, wg_ref, bg_ref, wo_ref, gpost_ref, out_ref, merged_ref):
    l0, l1, l2 = l0_ref[...], l1_ref[...], l2_ref[...]
    mx = jnp.maximum(jnp.maximum(l0, l1), l2)
    e0, e1, e2 = jnp.exp(l0 - mx), jnp.exp(l1 - mx), jnp.exp(l2 - mx)
    den = e0 + e1 + e2
    o = (e0 * o0_ref[...].astype(F32) + e1 * o1_ref[...].astype(F32) + e2 * o2_ref[...].astype(F32)) / den
    o = o.astype(BF16)
    o = jnp.concatenate([o[hd] for hd in range(HEADS_PER_GROUP)], axis=-1)
    yc = yc_ref[...]
    u = u_ref[...]

    tn = MIX_COL_CHUNK
    for c in range(D_MODEL // tn):
        lo = c * tn
        g_conv = _sigmoid(_dot(u, wg_ref[:, lo:lo + tn]) + bg_ref[:, lo:lo + tn])
        g_attn = _sigmoid(_dot(u, wg_ref[:, D_MODEL + lo:D_MODEL + lo + tn])
                          + bg_ref[:, D_MODEL + lo:D_MODEL + lo + tn])
        y_conv = _dot(yc, wco_ref[:, lo:lo + tn])
        y_attn = _dot(o, wao_ref[:, lo:lo + tn])
        merged_ref[:, lo:lo + tn] = (g_conv * y_conv + g_attn * y_attn).astype(BF16)

    mix = _dot(merged_ref[...], wo_ref[...])
    out_ref[...] = h_ref[...] + _rms_norm(mix, gpost_ref[...])


def _mix_out(h, u, yc, o_groups, lse_groups, w_conv_out, w_attn_out, w_gate, b_gate, w_o, g_post):
    b, s, _ = h.shape
    tm = MIX_ROW_TILE

    def rows(width):
        return pl.BlockSpec((None, tm, width), lambda bi, i: (bi, i, 0))

    heads = pl.BlockSpec((None, HEADS_PER_GROUP, tm, HEAD_DIM), lambda bi, i: (bi, 0, i, 0))

    def resident(shape):
        return pl.BlockSpec(shape, lambda bi, i: (0, 0), pipeline_mode=pl.Buffered(1))

    return pl.pallas_call(
        _mix_out_kernel,
        grid=(b, s // tm),
        in_specs=[
            rows(D_MODEL), rows(D_MODEL), rows(D_CONV),
            heads, heads, heads,
            heads, heads, heads,
            resident((D_CONV, D_MODEL)),
            resident((D_ATTN, D_MODEL)),
            resident((D_MODEL, 2 * D_MODEL)),
            resident((1, 2 * D_MODEL)),
            resident((D_MODEL, D_MODEL)),
            resident((1, D_MODEL)),
        ],
        out_specs=rows(D_MODEL),
        out_shape=jax.ShapeDtypeStruct((b, s, D_MODEL), F32),
        scratch_shapes=[pltpu.VMEM((tm, D_MODEL), BF16)],
        compiler_params=_params("parallel", "parallel"),
        name="mix_out",
    )(h, u, yc, *o_groups, *lse_groups, w_conv_out, w_attn_out, w_gate, b_gate, w_o, g_post)


def _encoder_layer(x, p):
    b, s, _ = x.shape
    m = b * s
    h, u = _ffn(x.reshape(m, D_MODEL), p["ffn1_norm_pre"], p["ffn1_w1"], p["ffn1_w3"], p["ffn1_w2"],
                p["ffn1_norm_post"], p["mix_norm_pre"])
    u = u.reshape(b, s, D_MODEL)
    natural = _proj(u, p["w_in_natural"], 1)
    yc = _conv(natural.reshape(b, s, D_NATURAL), p["conv_w"], p["conv_b"])
    o_groups, lse_groups = [], []
    for g in range(N_GROUPS):
        if g == 0:
            o_g, lse_g = _attn_group(natural, g, 3 * D_CONV // D_ATTN)
        else:
            o_g, lse_g = _attn_group(_proj(u, p["w_in_groups"][g], ATTN_DILATIONS[g]), g, 0)
        o_groups.append(o_g)
        lse_groups.append(lse_g)
    h = _mix_out(h.reshape(b, s, D_MODEL), u, yc, o_groups, lse_groups, p["w_conv_out"], p["w_attn_out"],
                 p["w_gate"], p["b_gate"], p["w_o"], p["mix_norm_post"])
    y = _ffn(h.reshape(m, D_MODEL), p["ffn2_norm_pre"], p["ffn2_w1"], p["ffn2_w3"], p["ffn2_w2"],
             p["ffn2_norm_post"])
    return y.reshape(b, s, D_MODEL)


_MATMUL_WEIGHTS = ("ffn1_w1", "ffn1_w3", "ffn1_w2", "w_in", "w_conv_out", "w_attn_out", "w_gate", "w_o",
                   "ffn2_w1", "ffn2_w3", "ffn2_w2")
_ROW_VECTORS = ("ffn1_norm_pre", "ffn1_norm_post", "mix_norm_pre", "conv_b", "b_gate", "mix_norm_post",
                "ffn2_norm_pre", "ffn2_norm_post")


def _group_qkv_columns(w_in, group):
    base = 3 * D_CONV + group * D_ATTN
    return [w_in[:, base + i * D_QKV: base + i * D_QKV + D_ATTN] for i in range(3)]


def kernel(x_prompt, x_sample, ffn1_norm_pre, ffn1_w1, ffn1_w3, ffn1_w2, ffn1_norm_post, mix_norm_pre, w_in, conv_w, conv_b, w_conv_out, w_attn_out, w_gate, b_gate, w_o, mix_norm_post, ffn2_norm_pre, ffn2_w1, ffn2_w3, ffn2_w2, ffn2_norm_post):
    stacked = dict(
        ffn1_norm_pre=ffn1_norm_pre, ffn1_w1=ffn1_w1, ffn1_w3=ffn1_w3, ffn1_w2=ffn1_w2,
        ffn1_norm_post=ffn1_norm_post, mix_norm_pre=mix_norm_pre, w_in=w_in, conv_w=conv_w, conv_b=conv_b,
        w_conv_out=w_conv_out, w_attn_out=w_attn_out, w_gate=w_gate, b_gate=b_gate, w_o=w_o,
        mix_norm_post=mix_norm_post, ffn2_norm_pre=ffn2_norm_pre, ffn2_w1=ffn2_w1, ffn2_w3=ffn2_w3,
        ffn2_w2=ffn2_w2, ffn2_norm_post=ffn2_norm_post)
    depth = w_in.shape[0]
    y_prompt, y_sample = x_prompt, x_sample
    for layer in range(depth):
        p = {}
        for name, w in stacked.items():
            w = w[layer]
            if name in _MATMUL_WEIGHTS:
                w = w.astype(BF16)
            elif name in _ROW_VECTORS:
                w = w.reshape(1, -1)
            p[name] = w
        w_in_l = p.pop("w_in")
        p["w_in_natural"] = jnp.concatenate([w_in_l[:, :3 * D_CONV]] + _group_qkv_columns(w_in_l, 0), axis=1)
        p["w_in_groups"] = {g: jnp.concatenate(_group_qkv_columns(w_in_l, g), axis=1) for g in range(1, N_GROUPS)}
        y_prompt = _encoder_layer(y_prompt, p)
        y_sample = _encoder_layer(y_sample, p)
    return (y_prompt, y_sample)
```

```python
import functools

import jax
import jax.numpy as jnp
from jax import lax
from jax.experimental import pallas as pl
from jax.experimental.pallas import tpu as pltpu

D_MODEL = 2048
D_FF = 5632
D_CONV = 1024
N_GROUPS = 3
HEADS_PER_GROUP = 4
N_HEADS = N_GROUPS * HEADS_PER_GROUP
HEAD_DIM = 128
D_ATTN = HEADS_PER_GROUP * HEAD_DIM
D_QKV = N_GROUPS * D_ATTN
ATTN_WINDOWS = (128, 512, 2048)
ATTN_DILATIONS = (1, 4, 16)
D_IN = 3 * D_CONV + 3 * D_QKV
D_NATURAL = 3 * D_CONV + 3 * D_ATTN
RMS_EPS = 1e-6
FFN_RESIDUAL_WEIGHT = 0.5
LOG2_E = 1.4426950408889634
LN_2 = 0.6931471805599453

V7X_VMEM_BYTES = 64 * 1024 * 1024
VMEM_LIMIT_BYTES = V7X_VMEM_BYTES - 8 * 1024 * 1024

FFN_UP_ROW_TILE = 1024
FFN_DOWN_ROW_TILE = 256
FFN_FF_TILE = 512
PROJ_ROW_TILE = 1024
PROJ_COL_TILE = 1536
MIX_ROW_TILE = 256
MIX_COL_CHUNK = 512
CONV_COL_TILE = 256
ATTN_Q_TILE = 128
ATTN_LOOKAHEAD = 2

LANES = 128
MXU_COLS = 256
MAX_SUBLANE_STRIDE = 4

BF16 = jnp.bfloat16
F32 = jnp.float32


def _rms_norm(x, g):
    return x * lax.rsqrt(jnp.mean(x * x, axis=-1, keepdims=True) + RMS_EPS) * g


def _dot(a, b):
    return jnp.dot(a, b, preferred_element_type=F32)


def _sigmoid(x):
    return 1.0 / (1.0 + jnp.exp(-x))


def _params(*semantics):
    return pltpu.CompilerParams(dimension_semantics=semantics, vmem_limit_bytes=VMEM_LIMIT_BYTES)


def _ffn_up_kernel(x_ref, gpre_ref, w1_ref, w3_ref, hidden_ref):
    x = x_ref[...]
    rinv = lax.rsqrt(jnp.mean(x * x, axis=-1, keepdims=True) + RMS_EPS)
    xg = (x * gpre_ref[...]).astype(BF16)
    a = _dot(xg, w1_ref[...]) * rinv
    c = _dot(xg, w3_ref[...]) * rinv
    hidden_ref[...] = (a * _sigmoid(a) * c).astype(BF16)


def _ffn_down_kernel(*refs, emit_normed):
    if emit_normed:
        hidden_ref, w2_ref, x_ref, gpost_ref, gnext_ref, y_ref, u_ref = refs
    else:
        hidden_ref, w2_ref, x_ref, gpost_ref, y_ref = refs
    y = x_ref[...] + _rms_norm(_dot(hidden_ref[...], w2_ref[...]), gpost_ref[...])
    y_ref[...] = y
    if emit_normed:
        u_ref[...] = _rms_norm(y, gnext_ref[...]).astype(BF16)


def _ffn(x, g_pre, w1, w3, w2, g_post, g_next=None):
    m = x.shape[0]
    tm, tf = FFN_UP_ROW_TILE, FFN_FF_TILE
    hidden = pl.pallas_call(
        _ffn_up_kernel,
        grid=(m // tm, D_FF // tf),
        in_specs=[
            pl.BlockSpec((tm, D_MODEL), lambda i, j: (i, 0)),
            pl.BlockSpec((1, D_MODEL), lambda i, j: (0, 0)),
            pl.BlockSpec((D_MODEL, tf), lambda i, j: (0, j)),
            pl.BlockSpec((D_MODEL, tf), lambda i, j: (0, j)),
        ],
        out_specs=pl.BlockSpec((tm, tf), lambda i, j: (i, j)),
        out_shape=jax.ShapeDtypeStruct((m, D_FF), BF16),
        compiler_params=_params("parallel", "arbitrary"),
        name="ffn_up",
    )(x, g_pre, w1, w3)

    tm = FFN_DOWN_ROW_TILE
    emit_normed = g_next is not None
    row_spec = pl.BlockSpec((tm, D_MODEL), lambda i: (i, 0))
    gain_spec = pl.BlockSpec((1, D_MODEL), lambda i: (0, 0))
    in_specs = [pl.BlockSpec((tm, D_FF), lambda i: (i, 0)),
                pl.BlockSpec((D_FF, D_MODEL), lambda i: (0, 0), pipeline_mode=pl.Buffered(1)),
                row_spec, gain_spec]
    args = [hidden, w2, x, FFN_RESIDUAL_WEIGHT * g_post]
    out_specs = [row_spec]
    out_shape = [jax.ShapeDtypeStruct((m, D_MODEL), F32)]
    if emit_normed:
        in_specs.append(gain_spec)
        args.append(g_next)
        out_specs.append(row_spec)
        out_shape.append(jax.ShapeDtypeStruct((m, D_MODEL), BF16))
    outs = pl.pallas_call(
        functools.partial(_ffn_down_kernel, emit_normed=emit_normed),
        grid=(m // tm,),
        in_specs=in_specs,
        out_specs=out_specs,
        out_shape=out_shape,
        compiler_params=_params("parallel"),
        name="ffn_down",
    )(*args)
    return outs if emit_normed else outs[0]


def _proj_kernel(u_ref, w_ref, o_ref):
    o_ref[...] = _dot(u_ref[...], w_ref[...]).astype(BF16)


def _proj(u, w):
    b, s, _ = u.shape
    n = w.shape[1]
    tm, tn = PROJ_ROW_TILE, PROJ_COL_TILE
    return pl.pallas_call(
        _proj_kernel,
        grid=(b, s // tm, n // tn),
        in_specs=[
            pl.BlockSpec((None, tm, D_MODEL), lambda bi, i, j: (bi, i, 0)),
            pl.BlockSpec((D_MODEL, tn), lambda bi, i, j: (0, j)),
        ],
        out_specs=pl.BlockSpec((None, tm, tn), lambda bi, i, j: (bi, i, j)),
        out_shape=jax.ShapeDtypeStruct((b, s, n), BF16),
        compiler_params=_params("parallel", "parallel", "arbitrary"),
        name="proj_d1",
    )(u, w)


def _dilated_proj_kernel(u_ref, w_ref, o_ref, *scratch, dilation):
    slabs = MXU_COLS // LANES
    rows = u_ref.shape[0] // dilation
    n_tiles = w_ref.shape[1] // MXU_COLS
    scratch, staging = scratch[:n_tiles], scratch[n_tiles:]

    def multiply(c):
        res = _dot(u_ref[...], w_ref[:, c * MXU_COLS:(c + 1) * MXU_COLS])
        for k in range(slabs):
            scratch[c][k] = res[:, k * LANES:(k + 1) * LANES]

    def regroup(c):
        for k in range(slabs):
            lo = c * MXU_COLS + k * LANES
            if dilation <= MAX_SUBLANE_STRIDE:
                for r in range(dilation):
                    o_ref[r, :, lo:lo + LANES] = scratch[c][k, pl.ds(r, rows, stride=dilation), :].astype(BF16)
                continue
            inner, outer = MAX_SUBLANE_STRIDE, dilation // MAX_SUBLANE_STRIDE
            tmp_ref = staging[c * slabs + k]
            for r0 in range(inner):
                tmp_ref[r0] = scratch[c][k, pl.ds(r0, rows * outer, stride=inner), :]
            for r0 in range(inner):
                for r1 in range(outer):
                    o_ref[r1 * inner + r0, :, lo:lo + LANES] = (
                        tmp_ref[r0, pl.ds(r1, rows, stride=outer), :].astype(BF16))

    for c in range(n_tiles + 1):
        if c < n_tiles:
            multiply(c)
        if c > 0:
            regroup(c - 1)


def _dilated_proj(u, w, dilation):
    b, s, _ = u.shape
    n = w.shape[1]
    tm, tn = PROJ_ROW_TILE, PROJ_COL_TILE
    scratch = [pltpu.VMEM((MXU_COLS // LANES, tm, LANES), F32)] * (tn // MXU_COLS)
    if dilation > MAX_SUBLANE_STRIDE:
        scratch += [pltpu.VMEM((MAX_SUBLANE_STRIDE, tm // MAX_SUBLANE_STRIDE, LANES), F32)] * (tn // LANES)
    return pl.pallas_call(
        functools.partial(_dilated_proj_kernel, dilation=dilation),
        grid=(b, s // tm, n // tn),
        in_specs=[
            pl.BlockSpec((None, tm, D_MODEL), lambda bi, i, j: (bi, i, 0)),
            pl.BlockSpec((D_MODEL, tn), lambda bi, i, j: (0, j)),
        ],
        out_specs=pl.BlockSpec((None, dilation, tm // dilation, tn), lambda bi, i, j: (bi, 0, i, j)),
        out_shape=jax.ShapeDtypeStruct((b, dilation, s // dilation, n), BF16),
        scratch_shapes=scratch,
        compiler_params=_params("parallel", "parallel", "arbitrary"),
        name=f"proj_d{dilation}",
    )(u, w)


def _conv_kernel(cb_ref, cc_ref, cx_ref, w_ref, b_ref, o_ref):
    s = cc_ref.shape[0]
    z = cc_ref[...].astype(F32) * cx_ref[...].astype(F32)
    t = lax.broadcasted_iota(jnp.int32, z.shape, 0)
    z_prev = jnp.where(t == 0, 0.0, pltpu.roll(z, 1, 0))
    z_next = jnp.where(t == s - 1, 0.0, pltpu.roll(z, s - 1, 0))
    w = w_ref[...]
    conv = z_prev * w[0:1] + z * w[1:2] + z_next * w[2:3] + b_ref[...]
    o_ref[...] = (cb_ref[...].astype(F32) * conv).astype(BF16)


def _conv(proj, conv_w, conv_b):
    b, s, _ = proj.shape
    tc = CONV_COL_TILE
    nc = D_CONV // tc
    return pl.pallas_call(
        _conv_kernel,
        grid=(b, nc),
        in_specs=[
            pl.BlockSpec((None, s, tc), lambda i, j: (i, 0, j)),
            pl.BlockSpec((None, s, tc), lambda i, j: (i, 0, nc + j)),
            pl.BlockSpec((None, s, tc), lambda i, j: (i, 0, 2 * nc + j)),
            pl.BlockSpec((3, tc), lambda i, j: (0, j)),
            pl.BlockSpec((1, tc), lambda i, j: (0, j)),
        ],
        out_specs=pl.BlockSpec((None, s, tc), lambda i, j: (i, 0, j)),
        out_shape=jax.ShapeDtypeStruct((b, s, D_CONV), BF16),
        compiler_params=_params("parallel", "parallel"),
        name="conv",
    )(proj, proj, proj, conv_w, conv_b)


def _attn_geometry(group, length):
    dilation = ATTN_DILATIONS[group]
    radius = ATTN_WINDOWS[group] // (2 * dilation)
    tq = ATTN_Q_TILE
    tk = min(tq + 2 * radius, length)
    offsets = (0,) if tk == length else (0, radius, tk - tq)
    return dilation, radius, tq, tk, offsets


def _attn_kernel(q_ref, k_ref, v_ref, o_ref, lse_ref, bias_ref, *scratch, group):
    length = q_ref.shape[1]
    dilation, radius, tq, tk, offsets = _attn_geometry(group, length)
    scale = HEAD_DIM ** -0.5 * LOG2_E
    o_nat_ref = o_ref if dilation == 1 else scratch[0]

    @pl.when(pl.program_id(0) == 0)
    def _():
        row = lax.broadcasted_iota(jnp.int32, (tq, tk), 0)
        col = lax.broadcasted_iota(jnp.int32, (tq, tk), 1)
        for vi, off in enumerate(offsets):
            delta = jnp.abs(col - row - off)
            dist = (dilation * delta).astype(F32)
            for h in range(HEADS_PER_GROUP):
                slope = 2.0 ** (-8.0 * (group * HEADS_PER_GROUP + h + 1) / N_HEADS)
                bias_ref[vi * HEADS_PER_GROUP + h] = jnp.where(delta <= radius, -slope * LOG2_E * dist, -jnp.inf)

    def key_start(q0):
        return min(max(q0 - radius, 0), length - tk)

    def scores(r, q0):
        k0 = key_start(q0)
        out = []
        for h in range(HEADS_PER_GROUP):
            cols = pl.ds(h * HEAD_DIM, HEAD_DIM)
            q = q_ref[r, pl.ds(q0, tq), cols]
            k = k_ref[r, pl.ds(k0, tk), cols]
            out.append(lax.dot_general(q, k, (((1,), (1,)), ((), ())), preferred_element_type=F32))
        return out

    def softmax_pv(r, q0, s_heads):
        k0 = key_start(q0)
        vi = offsets.index(q0 - k0)
        out_rows = pl.ds(q0, tq) if dilation == 1 else pl.ds(q0 * dilation + r, tq, stride=dilation)
        for h, s in enumerate(s_heads):
            v = v_ref[r, pl.ds(k0, tk), pl.ds(h * HEAD_DIM, HEAD_DIM)]
            s = s * scale + bias_ref[vi * HEADS_PER_GROUP + h]
            m = jnp.max(s, axis=-1, keepdims=True)
            p = jnp.exp2(s - m)
            l = jnp.sum(p, axis=-1, keepdims=True)
            o = _dot(p.astype(BF16), v) / l
            o_nat_ref[h, out_rows, :] = o.astype(o_nat_ref.dtype)
            lse_ref[h, out_rows, :] = jnp.broadcast_to(m * LN_2 + jnp.log(l), (tq, HEAD_DIM))

    tiles = [(r, it * tq) for r in range(dilation) for it in range(length // tq)]
    in_flight = []
    for idx in range(len(tiles) + ATTN_LOOKAHEAD):
        if idx < len(tiles):
            in_flight.append(scores(*tiles[idx]))
        if idx >= ATTN_LOOKAHEAD:
            softmax_pv(*tiles[idx - ATTN_LOOKAHEAD], in_flight.pop(0))

    if dilation != 1:
        o_ref[...] = o_nat_ref[...].astype(BF16)


def _attn_group(qkv, group, first_col_block):
    b, d, length, _ = qkv.shape
    s = d * length
    _, _, tq, tk, offsets = _attn_geometry(group, length)
    head_major = (HEADS_PER_GROUP, s, HEAD_DIM)

    def in_spec(blk):
        return pl.BlockSpec((None, d, length, D_ATTN), lambda i: (i, 0, 0, first_col_block + blk))

    out_spec = pl.BlockSpec((None,) + head_major, lambda i: (i, 0, 0, 0))
    scratch = [pltpu.VMEM((len(offsets) * HEADS_PER_GROUP, tq, tk), F32)]
    if d != 1:
        scratch.append(pltpu.VMEM(head_major, F32))
    return pl.pallas_call(
        functools.partial(_attn_kernel, group=group),
        grid=(b,),
        in_specs=[in_spec(0), in_spec(1), in_spec(2)],
        out_specs=[out_spec, out_spec],
        out_shape=[jax.ShapeDtypeStruct((b,) + head_major, BF16), jax.ShapeDtypeStruct((b,) + head_major, F32)],
        scratch_shapes=scratch,
        compiler_params=_params("arbitrary"),
        name=f"attn_g{group}",
    )(qkv, qkv, qkv)


def _mix_out_kernel(h_ref, u_ref, yc_ref, o0_ref, o1_ref, o2_ref, l0_ref, l1_ref, l2_ref,
                    wco_ref, wao_ref, wg_ref, bg_ref, wo_ref, gpost_ref, out_ref, merged_ref):
    l0, l1, l2 = l0_ref[...], l1_ref[...], l2_ref[...]
    mx = jnp.maximum(jnp.maximum(l0, l1), l2)
    e0, e1, e2 = jnp.exp(l0 - mx), jnp.exp(l1 - mx), jnp.exp(l2 - mx)
    den = e0 + e1 + e2
    o = (e0 * o0_ref[...].astype(F32) + e1 * o1_ref[...].astype(F32) + e2 * o2_ref[...].astype(F32)) / den
    o = o.astype(BF16)
    o = jnp.concatenate([o[hd] for hd in range(HEADS_PER_GROUP)], axis=-1)
    yc = yc_ref[...]
    u = u_ref[...]

    tn = MIX_COL_CHUNK
    for c in range(D_MODEL // tn):
        lo = c * tn
        g_conv = _sigmoid(_dot(u, wg_ref[:, lo:lo + tn]) + bg_ref[:, lo:lo + tn])
        g_attn = _sigmoid(_dot(u, wg_ref[:, D_MODEL + lo:D_MODEL + lo + tn])
                          + bg_ref[:, D_MODEL + lo:D_MODEL + lo + tn])
        y_conv = _dot(yc, wco_ref[:, lo:lo + tn])
        y_attn = _dot(o, wao_ref[:, lo:lo + tn])
        merged_ref[:, lo:lo + tn] = (g_conv * y_conv + g_attn * y_attn).astype(BF16)

    mix = _dot(merged_ref[...], wo_ref[...])
    out_ref[...] = h_ref[...] + _rms_norm(mix, gpost_ref[...])


def _mix_out(h, u, yc, o_groups, lse_groups, w_conv_out, w_attn_out, w_gate, b_gate, w_o, g_post):
    b, s, _ = h.shape
    tm = MIX_ROW_TILE

    def rows(width):
        return pl.BlockSpec((None, tm, width), lambda bi, i: (bi, i, 0))

    heads = pl.BlockSpec((None, HEADS_PER_GROUP, tm, HEAD_DIM), lambda bi, i: (bi, 0, i, 0))

    def resident(shape):
        return pl.BlockSpec(shape, lambda bi, i: (0, 0), pipeline_mode=pl.Buffered(1))

    return pl.pallas_call(
        _mix_out_kernel,
        grid=(b, s // tm),
        in_specs=[
            rows(D_MODEL), rows(D_MODEL), rows(D_CONV),
            heads, heads, heads,
            heads, heads, heads,
            resident((D_CONV, D_MODEL)),
            resident((D_ATTN, D_MODEL)),
            resident((D_MODEL, 2 * D_MODEL)),
            resident((1, 2 * D_MODEL)),
            resident((D_MODEL, D_MODEL)),
            resident((1, D_MODEL)),
        ],
        out_specs=rows(D_MODEL),
        out_shape=jax.ShapeDtypeStruct((b, s, D_MODEL), F32),
        scratch_shapes=[pltpu.VMEM((tm, D_MODEL), BF16)],
        compiler_params=_params("parallel", "parallel"),
        name="mix_out",
    )(h, u, yc, *o_groups, *lse_groups, w_conv_out, w_attn_out, w_gate, b_gate, w_o, g_post)


def _encoder_layer(x, p):
    b, s, _ = x.shape
    m = b * s
    h, u = _ffn(x.reshape(m, D_MODEL), p["ffn1_norm_pre"], p["ffn1_w1"], p["ffn1_w3"], p["ffn1_w2"],
                p["ffn1_norm_post"], p["mix_norm_pre"])
    h = h.reshape(b, s, D_MODEL)
    u = u.reshape(b, s, D_MODEL)
    natural = _proj(u, p["w_in_natural"])
    yc = _conv(natural, p["conv_w"], p["conv_b"])
    o_groups, lse_groups = [], []
    for g in range(N_GROUPS):
        if g == 0:
            o_g, lse_g = _attn_group(natural.reshape(b, 1, s, D_NATURAL), g, 3 * D_CONV // D_ATTN)
        else:
            o_g, lse_g = _attn_group(_dilated_proj(u, p["w_in_groups"][g], ATTN_DILATIONS[g]), g, 0)
        o_groups.append(o_g)
        lse_groups.append(lse_g)
    h = _mix_out(h, u, yc, o_groups, lse_groups, p["w_conv_out"], p["w_attn_out"],
                 p["w_gate"], p["b_gate"], p["w_o"], p["mix_norm_post"])
    y = _ffn(h.reshape(m, D_MODEL), p["ffn2_norm_pre"], p["ffn2_w1"], p["ffn2_w3"], p["ffn2_w2"],
             p["ffn2_norm_post"])
    return y.reshape(b, s, D_MODEL)


_MATMUL_WEIGHTS = ("ffn1_w1", "ffn1_w3", "ffn1_w2", "w_in", "w_conv_out", "w_attn_out", "w_gate", "w_o",
                   "ffn2_w1", "ffn2_w3", "ffn2_w2")
_ROW_VECTORS = ("ffn1_norm_pre", "ffn1_norm_post", "mix_norm_pre", "conv_b", "b_gate", "mix_norm_post",
                "ffn2_norm_pre", "ffn2_norm_post")


def _group_qkv_columns(w_in, group):
    base = 3 * D_CONV + group * D_ATTN
    return [w_in[:, base + i * D_QKV: base + i * D_QKV + D_ATTN] for i in range(3)]


def kernel(x_prompt, x_sample, ffn1_norm_pre, ffn1_w1, ffn1_w3, ffn1_w2, ffn1_norm_post, mix_norm_pre, w_in, conv_w, conv_b, w_conv_out, w_attn_out, w_gate, b_gate, w_o, mix_norm_post, ffn2_norm_pre, ffn2_w1, ffn2_w3, ffn2_w2, ffn2_norm_post):
    stacked = dict(
        ffn1_norm_pre=ffn1_norm_pre, ffn1_w1=ffn1_w1, ffn1_w3=ffn1_w3, ffn1_w2=ffn1_w2,
        ffn1_norm_post=ffn1_norm_post, mix_norm_pre=mix_norm_pre, w_in=w_in, conv_w=conv_w, conv_b=conv_b,
        w_conv_out=w_conv_out, w_attn_out=w_attn_out, w_gate=w_gate, b_gate=b_gate, w_o=w_o,
        mix_norm_post=mix_norm_post, ffn2_norm_pre=ffn2_norm_pre, ffn2_w1=ffn2_w1, ffn2_w3=ffn2_w3,
        ffn2_w2=ffn2_w2, ffn2_norm_post=ffn2_norm_post)
    depth = w_in.shape[0]
    y_prompt, y_sample = x_prompt, x_sample
    for layer in range(depth):
        p = {}
        for name, w in stacked.items():
            w = w[layer]
            if name in _MATMUL_WEIGHTS:
                w = w.astype(BF16)
            elif name in _ROW_VECTORS:
                w = w.reshape(1, -1)
            p[name] = w
        w_in_l = p.pop("w_in")
        p["w_in_natural"] = jnp.concatenate([w_in_l[:, :3 * D_CONV]] + _group_qkv_columns(w_in_l, 0), axis=1)
        p["w_in_groups"] = {g: jnp.concatenate(_group_qkv_columns(w_in_l, g), axis=1) for g in range(1, N_GROUPS)}
        y_prompt = _encoder_layer(y_prompt, p)
        y_sample = _encoder_layer(y_sample, p)
    return (y_prompt, y_sample)
```

```python
import functools

import jax
import jax.numpy as jnp
from jax import lax
from jax.experimental import pallas as pl
from jax.experimental.pallas import tpu as pltpu

D_MODEL = 2048
D_FF = 5632
D_CONV = 1024
CONV_WIDTH = 3
N_GROUPS = 3
HEADS_PER_GROUP = 4
N_HEADS = N_GROUPS * HEADS_PER_GROUP
HEAD_DIM = 128
D_ATTN = HEADS_PER_GROUP * HEAD_DIM
D_QKV = N_GROUPS * D_ATTN
ATTN_WINDOWS = (128, 512, 2048)
ATTN_DILATIONS = (1, 4, 16)
D_IN = 3 * D_CONV + 3 * D_QKV
D_NATURAL = 3 * D_CONV + 3 * D_ATTN
RMS_EPS = 1e-6
FFN_RESIDUAL_WEIGHT = 0.5
LOG2_E = 1.4426950408889634
LN_2 = 0.6931471805599453

V7X_VMEM_BYTES = 64 * 1024 * 1024
VMEM_LIMIT_BYTES = V7X_VMEM_BYTES - 8 * 1024 * 1024

FFN_UP_ROW_TILE = 1024
FFN_DOWN_ROW_TILE = 256
FFN_FF_TILE = 512
PROJ_ROW_TILE = 1024
PROJ_COL_TILE = 1536
MIX_ROW_TILE = 256
MIX_COL_CHUNK = 512
CAST_BLOCK_BYTES = 4 * 1024 * 1024
ATTN_Q_TILE = 128
ATTN_LOOKAHEAD = 2

LANES = 128
MXU_COLS = 256
HALO_ROWS = 16
MAX_SUBLANE_STRIDE = 4

BF16 = jnp.bfloat16
F32 = jnp.float32


def _rms_norm(x, g):
    return x * lax.rsqrt(jnp.mean(x * x, axis=-1, keepdims=True) + RMS_EPS) * g


def _dot(a, b):
    return jnp.dot(a, b, preferred_element_type=F32)


def _sigmoid(x):
    return 1.0 / (1.0 + jnp.exp(-x))


def _params(*semantics):
    return pltpu.CompilerParams(dimension_semantics=semantics, vmem_limit_bytes=VMEM_LIMIT_BYTES)


def _ffn_up_kernel(x_ref, gpre_ref, w1_ref, w3_ref, hidden_ref):
    x = x_ref[...]
    rinv = lax.rsqrt(jnp.mean(x * x, axis=-1, keepdims=True) + RMS_EPS)
    xg = (x * gpre_ref[...]).astype(BF16)
    a = _dot(xg, w1_ref[...]) * rinv
    c = _dot(xg, w3_ref[...]) * rinv
    hidden_ref[...] = (a * _sigmoid(a) * c).astype(BF16)


def _ffn_down_kernel(*refs, emit_normed):
    if emit_normed:
        hidden_ref, w2_ref, x_ref, gpost_ref, gnext_ref, y_ref, u_ref = refs
    else:
        hidden_ref, w2_ref, x_ref, gpost_ref, y_ref = refs
    y = x_ref[...] + _rms_norm(_dot(hidden_ref[...], w2_ref[...]), gpost_ref[...])
    y_ref[...] = y
    if emit_normed:
        u_ref[...] = _rms_norm(y, gnext_ref[...]).astype(BF16)


def _ffn(x, g_pre, w1, w3, w2, g_post, g_next=None):
    m = x.shape[0]
    tm, tf = FFN_UP_ROW_TILE, FFN_FF_TILE
    hidden = pl.pallas_call(
        _ffn_up_kernel,
        grid=(m // tm, D_FF // tf),
        in_specs=[
            pl.BlockSpec((tm, D_MODEL), lambda i, j: (i, 0)),
            pl.BlockSpec((1, D_MODEL), lambda i, j: (0, 0)),
            pl.BlockSpec((D_MODEL, tf), lambda i, j: (0, j)),
            pl.BlockSpec((D_MODEL, tf), lambda i, j: (0, j)),
        ],
        out_specs=pl.BlockSpec((tm, tf), lambda i, j: (i, j)),
        out_shape=jax.ShapeDtypeStruct((m, D_FF), BF16),
        compiler_params=_params("parallel", "arbitrary"),
        name="ffn_up",
    )(x, g_pre, w1, w3)

    tm = FFN_DOWN_ROW_TILE
    emit_normed = g_next is not None
    row_spec = pl.BlockSpec((tm, D_MODEL), lambda i: (i, 0))
    gain_spec = pl.BlockSpec((1, D_MODEL), lambda i: (0, 0))
    in_specs = [pl.BlockSpec((tm, D_FF), lambda i: (i, 0)),
                pl.BlockSpec((D_FF, D_MODEL), lambda i: (0, 0), pipeline_mode=pl.Buffered(1)),
                row_spec, gain_spec]
    args = [hidden, w2, x, FFN_RESIDUAL_WEIGHT * g_post]
    out_specs = [row_spec]
    out_shape = [jax.ShapeDtypeStruct((m, D_MODEL), F32)]
    if emit_normed:
        in_specs.append(gain_spec)
        args.append(g_next)
        out_specs.append(row_spec)
        out_shape.append(jax.ShapeDtypeStruct((m, D_MODEL), BF16))
    outs = pl.pallas_call(
        functools.partial(_ffn_down_kernel, emit_normed=emit_normed),
        grid=(m // tm,),
        in_specs=in_specs,
        out_specs=out_specs,
        out_shape=out_shape,
        compiler_params=_params("parallel"),
        name="ffn_down",
    )(*args)
    return outs if emit_normed else outs[0]


def _proj_kernel(u_ref, w_ref, o_ref):
    o_ref[...] = _dot(u_ref[...], w_ref[...]).astype(BF16)


def _proj(u, w, n):
    b, s, _ = u.shape
    tm, tn = PROJ_ROW_TILE, PROJ_COL_TILE
    return pl.pallas_call(
        _proj_kernel,
        grid=(b, s // tm, n // tn),
        in_specs=[
            pl.BlockSpec((None, tm, D_MODEL), lambda bi, i, j: (bi, i, 0)),
            pl.BlockSpec((D_MODEL, tn), lambda bi, i, j: (0, j)),
        ],
        out_specs=pl.BlockSpec((None, tm, tn), lambda bi, i, j: (bi, i, j)),
        out_shape=jax.ShapeDtypeStruct((b, s, n), BF16),
        compiler_params=_params("parallel", "parallel", "arbitrary"),
        name="proj_d1",
    )(u, w)


def _dilated_proj_kernel(u_ref, w_ref, o_ref, *scratch, dilation):
    slabs = MXU_COLS // LANES
    rows = u_ref.shape[0] // dilation
    n_tiles = w_ref.shape[1] // MXU_COLS
    scratch, staging = scratch[:n_tiles], scratch[n_tiles:]

    def multiply(c):
        res = _dot(u_ref[...], w_ref[:, c * MXU_COLS:(c + 1) * MXU_COLS])
        for k in range(slabs):
            scratch[c][k] = res[:, k * LANES:(k + 1) * LANES]

    def regroup(c):
        for k in range(slabs):
            lo = c * MXU_COLS + k * LANES
            if dilation <= MAX_SUBLANE_STRIDE:
                for r in range(dilation):
                    o_ref[r, :, lo:lo + LANES] = scratch[c][k, pl.ds(r, rows, stride=dilation), :].astype(BF16)
                continue
            inner, outer = MAX_SUBLANE_STRIDE, dilation // MAX_SUBLANE_STRIDE
            tmp_ref = staging[c * slabs + k]
            for r0 in range(inner):
                tmp_ref[r0] = scratch[c][k, pl.ds(r0, rows * outer, stride=inner), :]
            for r0 in range(inner):
                for r1 in range(outer):
                    o_ref[r1 * inner + r0, :, lo:lo + LANES] = (
                        tmp_ref[r0, pl.ds(r1, rows, stride=outer), :].astype(BF16))

    for c in range(n_tiles + 1):
        if c < n_tiles:
            multiply(c)
        if c > 0:
            regroup(c - 1)


def _dilated_proj(u, w, first_col, n, dilation):
    b, s, _ = u.shape
    tm, tn = PROJ_ROW_TILE, PROJ_COL_TILE
    first_tile = first_col // tn
    scratch = [pltpu.VMEM((MXU_COLS // LANES, tm, LANES), F32)] * (tn // MXU_COLS)
    if dilation > MAX_SUBLANE_STRIDE:
        scratch += [pltpu.VMEM((MAX_SUBLANE_STRIDE, tm // MAX_SUBLANE_STRIDE, LANES), F32)] * (tn // LANES)
    return pl.pallas_call(
        functools.partial(_dilated_proj_kernel, dilation=dilation),
        grid=(b, s // tm, n // tn),
        in_specs=[
            pl.BlockSpec((None, tm, D_MODEL), lambda bi, i, j: (bi, i, 0)),
            pl.BlockSpec((D_MODEL, tn), lambda bi, i, j: (0, first_tile + j)),
        ],
        out_specs=pl.BlockSpec((None, dilation, tm // dilation, tn), lambda bi, i, j: (bi, 0, i, j)),
        out_shape=jax.ShapeDtypeStruct((b, dilation, s // dilation, n), BF16),
        scratch_shapes=scratch,
        compiler_params=_params("parallel", "parallel", "arbitrary"),
        name=f"proj_d{dilation}",
    )(u, w)


def _attn_geometry(group, length):
    dilation = ATTN_DILATIONS[group]
    radius = ATTN_WINDOWS[group] // (2 * dilation)
    tq = ATTN_Q_TILE
    tk = min(tq + 2 * radius, length)
    offsets = (0,) if tk == length else (0, radius, tk - tq)
    return dilation, radius, tq, tk, offsets


def _attn_kernel(q_ref, k_ref, v_ref, o_ref, lse_ref, bias_ref, *scratch, group):
    length = q_ref.shape[1]
    dilation, radius, tq, tk, offsets = _attn_geometry(group, length)
    scale = HEAD_DIM ** -0.5 * LOG2_E
    o_nat_ref = o_ref if dilation == 1 else scratch[0]

    @pl.when(pl.program_id(0) == 0)
    def _():
        row = lax.broadcasted_iota(jnp.int32, (tq, tk), 0)
        col = lax.broadcasted_iota(jnp.int32, (tq, tk), 1)
        for vi, off in enumerate(offsets):
            delta = jnp.abs(col - row - off)
            dist = (dilation * delta).astype(F32)
            for h in range(HEADS_PER_GROUP):
                slope = 2.0 ** (-8.0 * (group * HEADS_PER_GROUP + h + 1) / N_HEADS)
                bias_ref[vi * HEADS_PER_GROUP + h] = jnp.where(delta <= radius, -slope * LOG2_E * dist, -jnp.inf)

    def key_start(q0):
        return min(max(q0 - radius, 0), length - tk)

    def scores(r, q0):
        k0 = key_start(q0)
        out = []
        for h in range(HEADS_PER_GROUP):
            cols = pl.ds(h * HEAD_DIM, HEAD_DIM)
            q = q_ref[r, pl.ds(q0, tq), cols]
            k = k_ref[r, pl.ds(k0, tk), cols]
            out.append(lax.dot_general(q, k, (((1,), (1,)), ((), ())), preferred_element_type=F32))
        return out

    def softmax_pv(r, q0, s_heads):
        k0 = key_start(q0)
        vi = offsets.index(q0 - k0)
        out_rows = pl.ds(q0, tq) if dilation == 1 else pl.ds(q0 * dilation + r, tq, stride=dilation)
        for h, s in enumerate(s_heads):
            v = v_ref[r, pl.ds(k0, tk), pl.ds(h * HEAD_DIM, HEAD_DIM)]
            s = s * scale + bias_ref[vi * HEADS_PER_GROUP + h]
            m = jnp.max(s, axis=-1, keepdims=True)
            p = jnp.exp2(s - m)
            l = jnp.sum(p, axis=-1, keepdims=True)
            o = _dot(p.astype(BF16), v) / l
            o_nat_ref[h, out_rows, :] = o.astype(o_nat_ref.dtype)
            lse_ref[h, out_rows, :] = jnp.broadcast_to(m * LN_2 + jnp.log(l), (tq, HEAD_DIM))

    tiles = [(r, it * tq) for r in range(dilation) for it in range(length // tq)]
    in_flight = []
    for idx in range(len(tiles) + ATTN_LOOKAHEAD):
        if idx < len(tiles):
            in_flight.append(scores(*tiles[idx]))
        if idx >= ATTN_LOOKAHEAD:
            softmax_pv(*tiles[idx - ATTN_LOOKAHEAD], in_flight.pop(0))

    if dilation != 1:
        o_ref[...] = o_nat_ref[...].astype(BF16)


def _attn_group(qkv, group, first_col_block):
    b, d, length, _ = qkv.shape
    s = d * length
    _, _, tq, tk, offsets = _attn_geometry(group, length)
    head_major = (HEADS_PER_GROUP, s, HEAD_DIM)

    def in_spec(blk):
        return pl.BlockSpec((None, d, length, D_ATTN), lambda i: (i, 0, 0, first_col_block + blk))

    out_spec = pl.BlockSpec((None,) + head_major, lambda i: (i, 0, 0, 0))
    scratch = [pltpu.VMEM((len(offsets) * HEADS_PER_GROUP, tq, tk), F32)]
    if d != 1:
        scratch.append(pltpu.VMEM(head_major, F32))
    return pl.pallas_call(
        functools.partial(_attn_kernel, group=group),
        grid=(b,),
        in_specs=[in_spec(0), in_spec(1), in_spec(2)],
        out_specs=[out_spec, out_spec],
        out_shape=[jax.ShapeDtypeStruct((b,) + head_major, BF16), jax.ShapeDtypeStruct((b,) + head_major, F32)],
        scratch_shapes=scratch,
        compiler_params=_params("arbitrary"),
        name=f"attn_g{group}",
    )(qkv, qkv, qkv)


def _gated_short_conv(cb_ref, cc_ref, cx_ref, cc_prev_ref, cx_prev_ref, cc_next_ref, cx_next_ref, w_ref, b_ref):
    i = pl.program_id(1)
    tm = cc_ref.shape[0]
    z = cc_ref[...].astype(F32) * cx_ref[...].astype(F32)
    z_before = (cc_prev_ref[...].astype(F32) * cx_prev_ref[...].astype(F32))[HALO_ROWS - 1:HALO_ROWS]
    z_after = (cc_next_ref[...].astype(F32) * cx_next_ref[...].astype(F32))[0:1]
    z_before = jnp.where(i == 0, 0.0, z_before)
    z_after = jnp.where(i == pl.num_programs(1) - 1, 0.0, z_after)
    t = lax.broadcasted_iota(jnp.int32, z.shape, 0)
    z_prev = jnp.where(t == 0, z_before, pltpu.roll(z, 1, 0))
    z_next = jnp.where(t == tm - 1, z_after, pltpu.roll(z, tm - 1, 0))
    w = w_ref[...]
    conv = z_prev * w[0:1] + z * w[1:2] + z_next * w[2:3] + b_ref[...]
    return (cb_ref[...].astype(F32) * conv).astype(BF16)


def _mix_out_kernel(h_ref, u_ref, cb_ref, cc_ref, cx_ref, cc_prev_ref, cx_prev_ref, cc_next_ref, cx_next_ref,
                    o0_ref, o1_ref, o2_ref, l0_ref, l1_ref, l2_ref, cw_ref, cbias_ref,
                    wco_ref, wao_ref, wg_ref, bg_ref, wo_ref, gpost_ref, out_ref, merged_ref):
    yc = _gated_short_conv(cb_ref, cc_ref, cx_ref, cc_prev_ref, cx_prev_ref, cc_next_ref, cx_next_ref,
                           cw_ref, cbias_ref)
    l0, l1, l2 = l0_ref[...], l1_ref[...], l2_ref[...]
    mx = jnp.maximum(jnp.maximum(l0, l1), l2)
    e0, e1, e2 = jnp.exp(l0 - mx), jnp.exp(l1 - mx), jnp.exp(l2 - mx)
    den = e0 + e1 + e2
    o = (e0 * o0_ref[...].astype(F32) + e1 * o1_ref[...].astype(F32) + e2 * o2_ref[...].astype(F32)) / den
    o = o.astype(BF16)
    o = jnp.concatenate([o[hd] for hd in range(HEADS_PER_GROUP)], axis=-1)
    u = u_ref[...]

    tn = MIX_COL_CHUNK
    chunks = [c * tn for c in range(D_MODEL // tn)]
    g_conv = [_sigmoid(_dot(u, wg_ref[:, lo:lo + tn]) + bg_ref[:, lo:lo + tn]) for lo in chunks]
    g_attn = [_sigmoid(_dot(u, wg_ref[:, D_MODEL + lo:D_MODEL + lo + tn]) + bg_ref[:, D_MODEL + lo:D_MODEL + lo + tn])
              for lo in chunks]
    for c, lo in enumerate(chunks):
        y_conv = _dot(yc, wco_ref[:, lo:lo + tn])
        y_attn = _dot(o, wao_ref[:, lo:lo + tn])
        merged_ref[:, lo:lo + tn] = (g_conv[c] * y_conv + g_attn[c] * y_attn).astype(BF16)

    mix = _dot(merged_ref[...], wo_ref[...])
    out_ref[...] = h_ref[...] + _rms_norm(mix, gpost_ref[...])


def _mix_out(h, u, natural, o_groups, lse_groups, conv_w, conv_b, w_conv_out, w_attn_out, w_gate, b_gate, w_o,
             g_post):
    b, s, _ = h.shape
    tm = MIX_ROW_TILE
    halo_blocks_per_tile = tm // HALO_ROWS
    last_halo_block = s // HALO_ROWS - 1

    def rows(width, col_block=0):
        return pl.BlockSpec((None, tm, width), lambda bi, i: (bi, i, col_block))

    def halo_before(col_block):
        return pl.BlockSpec((None, HALO_ROWS, D_CONV),
                            lambda bi, i: (bi, jnp.maximum(i * halo_blocks_per_tile - 1, 0), col_block))

    def halo_after(col_block):
        return pl.BlockSpec((None, HALO_ROWS, D_CONV),
                            lambda bi, i: (bi, jnp.minimum((i + 1) * halo_blocks_per_tile, last_halo_block), col_block))

    heads = pl.BlockSpec((None, HEADS_PER_GROUP, tm, HEAD_DIM), lambda bi, i: (bi, 0, i, 0))

    def resident(shape):
        return pl.BlockSpec(shape, lambda bi, i: (0, 0), pipeline_mode=pl.Buffered(1))

    return pl.pallas_call(
        _mix_out_kernel,
        grid=(b, s // tm),
        in_specs=[
            rows(D_MODEL), rows(D_MODEL),
            rows(D_CONV, 0), rows(D_CONV, 1), rows(D_CONV, 2),
            halo_before(1), halo_before(2), halo_after(1), halo_after(2),
            heads, heads, heads,
            heads, heads, heads,
            resident((CONV_WIDTH, D_CONV)),
            resident((1, D_CONV)),
            resident((D_CONV, D_MODEL)),
            resident((D_ATTN, D_MODEL)),
            resident((D_MODEL, 2 * D_MODEL)),
            resident((1, 2 * D_MODEL)),
            resident((D_MODEL, D_MODEL)),
            resident((1, D_MODEL)),
        ],
        out_specs=rows(D_MODEL),
        out_shape=jax.ShapeDtypeStruct((b, s, D_MODEL), F32),
        scratch_shapes=[pltpu.VMEM((tm, D_MODEL), BF16)],
        compiler_params=_params("parallel", "parallel"),
        name="mix_out",
    )(h, u, natural, natural, natural, natural, natural, natural, natural, *o_groups, *lse_groups,
      conv_w, conv_b, w_conv_out, w_attn_out, w_gate, b_gate, w_o, g_post)


def _encoder_layer(x, p):
    b, s, _ = x.shape
    m = b * s
    h, u = _ffn(x.reshape(m, D_MODEL), p["ffn1_norm_pre"], p["ffn1_w1"], p["ffn1_w3"], p["ffn1_w2"],
                p["ffn1_norm_post"], p["mix_norm_pre"])
    h = h.reshape(b, s, D_MODEL)
    u = u.reshape(b, s, D_MODEL)
    natural = _proj(u, p["w_in"], D_NATURAL)
    o_groups, lse_groups = [], []
    for g in range(N_GROUPS):
        if g == 0:
            o_g, lse_g = _attn_group(natural.reshape(b, 1, s, D_NATURAL), g, 3 * D_CONV // D_ATTN)
        else:
            qkv = _dilated_proj(u, p["w_in"], D_NATURAL + (g - 1) * 3 * D_ATTN, 3 * D_ATTN, ATTN_DILATIONS[g])
            o_g, lse_g = _attn_group(qkv, g, 0)
        o_groups.append(o_g)
        lse_groups.append(lse_g)
    h = _mix_out(h, u, natural, o_groups, lse_groups, p["conv_w"], p["conv_b"], p["w_conv_out"], p["w_attn_out"],
                 p["w_gate"], p["b_gate"], p["w_o"], p["mix_norm_post"])
    y = _ffn(h.reshape(m, D_MODEL), p["ffn2_norm_pre"], p["ffn2_w1"], p["ffn2_w3"], p["ffn2_w2"],
             p["ffn2_norm_post"])
    return y.reshape(b, s, D_MODEL)


_MATMUL_WEIGHTS = ("ffn1_w1", "ffn1_w3", "ffn1_w2", "w_in", "w_conv_out", "w_attn_out", "w_gate", "w_o",
                   "ffn2_w1", "ffn2_w3", "ffn2_w2")
_ROW_VECTORS = ("ffn1_norm_pre", "ffn1_norm_post", "mix_norm_pre", "conv_b", "b_gate", "mix_norm_post",
                "ffn2_norm_pre", "ffn2_norm_post")


def _cast_kernel(x_ref, o_ref):
    o_ref[...] = x_ref[...].astype(BF16)


def _cast_bf16(w):
    rows, cols = w.shape
    tr = rows
    while tr * cols * 4 > CAST_BLOCK_BYTES and tr % (2 * HALO_ROWS) == 0:
        tr //= 2
    spec = pl.BlockSpec((tr, cols), lambda i: (i, 0))
    return pl.pallas_call(
        _cast_kernel, grid=(rows // tr,), in_specs=[spec], out_specs=spec,
        out_shape=jax.ShapeDtypeStruct((rows, cols), BF16),
        compiler_params=_params("parallel"), name="cast_bf16",
    )(w)


def _cast_regroup_w_in(w_in):
    conv_blocks = 3 * D_CONV // D_ATTN

    def source_block(j):
        k = j - conv_blocks
        return jnp.where(j < conv_blocks, j, conv_blocks + (k % 3) * N_GROUPS + k // 3)

    return pl.pallas_call(
        _cast_kernel,
        grid=(D_IN // D_ATTN,),
        in_specs=[pl.BlockSpec((D_MODEL, D_ATTN), lambda j: (0, source_block(j)))],
        out_specs=pl.BlockSpec((D_MODEL, D_ATTN), lambda j: (0, j)),
        out_shape=jax.ShapeDtypeStruct((D_MODEL, D_IN), BF16),
        compiler_params=_params("parallel"), name="cast_w_in",
    )(w_in)


def kernel(x_prompt, x_sample, ffn1_norm_pre, ffn1_w1, ffn1_w3, ffn1_w2, ffn1_norm_post, mix_norm_pre, w_in, conv_w, conv_b, w_conv_out, w_attn_out, w_gate, b_gate, w_o, mix_norm_post, ffn2_norm_pre, ffn2_w1, ffn2_w3, ffn2_w2, ffn2_norm_post):
    stacked = dict(
        ffn1_norm_pre=ffn1_norm_pre, ffn1_w1=ffn1_w1, ffn1_w3=ffn1_w3, ffn1_w2=ffn1_w2,
        ffn1_norm_post=ffn1_norm_post, mix_norm_pre=mix_norm_pre, w_in=w_in, conv_w=conv_w, conv_b=conv_b,
        w_conv_out=w_conv_out, w_attn_out=w_attn_out, w_gate=w_gate, b_gate=b_gate, w_o=w_o,
        mix_norm_post=mix_norm_post, ffn2_norm_pre=ffn2_norm_pre, ffn2_w1=ffn2_w1, ffn2_w3=ffn2_w3,
        ffn2_w2=ffn2_w2, ffn2_norm_post=ffn2_norm_post)
    depth = w_in.shape[0]
    y_prompt, y_sample = x_prompt, x_sample
    for layer in range(depth):
        p = {}
        for name, w in stacked.items():
            w = w[layer]
            if name == "w_in":
                w = _cast_regroup_w_in(w)
            elif name in _MATMUL_WEIGHTS:
                w = _cast_bf16(w)
            elif name in _ROW_VECTORS:
                w = w.reshape(1, -1)
            p[name] = w
        y_prompt = _encoder_layer(y_prompt, p)
        y_sample = _encoder_layer(y_sample, p)
    return (y_prompt, y_sample)
```

```python
import functools

import jax
import jax.numpy as jnp
from jax import lax
from jax.experimental import pallas as pl
from jax.experimental.pallas import tpu as pltpu

D_MODEL = 2048
D_FF = 5632
D_CONV = 1024
CONV_WIDTH = 3
N_GROUPS = 3
HEADS_PER_GROUP = 4
N_HEADS = N_GROUPS * HEADS_PER_GROUP
HEAD_DIM = 128
D_ATTN = HEADS_PER_GROUP * HEAD_DIM
D_QKV = N_GROUPS * D_ATTN
ATTN_WINDOWS = (128, 512, 2048)
ATTN_DILATIONS = (1, 4, 16)
D_IN = 3 * D_CONV + 3 * D_QKV
D_NATURAL = 3 * D_CONV + 3 * D_ATTN
RMS_EPS = 1e-6
FFN_RESIDUAL_WEIGHT = 0.5
LOG2_E = 1.4426950408889634
LN_2 = 0.6931471805599453

V7X_VMEM_BYTES = 64 * 1024 * 1024
VMEM_LIMIT_BYTES = V7X_VMEM_BYTES - 8 * 1024 * 1024

FFN_UP_ROW_TILE = 1024
FFN_DOWN_ROW_TILE = 256
FFN_FF_TILE = 512
PROJ_ROW_TILE = 1024
PROJ_COL_TILE = 1536
MIX_ROW_TILE = 256
MIX_COL_CHUNK = 512
CAST_BLOCK_BYTES = 4 * 1024 * 1024
ATTN_Q_TILE = 128
ATTN_LOOKAHEAD = 1

LANES = 128
MXU_COLS = 256
HALO_ROWS = 16
MAX_SUBLANE_STRIDE = 4

BF16 = jnp.bfloat16
F32 = jnp.float32


def _rms_norm(x, g):
    return x * lax.rsqrt(jnp.mean(x * x, axis=-1, keepdims=True) + RMS_EPS) * g


def _dot(a, b):
    return jnp.dot(a, b, preferred_element_type=F32)


def _sigmoid(x):
    return 1.0 / (1.0 + jnp.exp(-x))


def _params(*semantics):
    return pltpu.CompilerParams(dimension_semantics=semantics, vmem_limit_bytes=VMEM_LIMIT_BYTES)


def _ffn_up_kernel(x_ref, gpre_ref, w1_ref, w3_ref, hidden_ref):
    x = x_ref[...]
    rinv = lax.rsqrt(jnp.mean(x * x, axis=-1, keepdims=True) + RMS_EPS)
    xg = (x * gpre_ref[...]).astype(BF16)
    a = _dot(xg, w1_ref[...]) * rinv
    c = _dot(xg, w3_ref[...]) * rinv
    hidden_ref[...] = (a * _sigmoid(a) * c).astype(BF16)


def _ffn_down_kernel(*refs, emit_normed):
    if emit_normed:
        hidden_ref, w2_ref, x_ref, gpost_ref, gnext_ref, y_ref, u_ref = refs
    else:
        hidden_ref, w2_ref, x_ref, gpost_ref, y_ref = refs
    y = x_ref[...] + _rms_norm(_dot(hidden_ref[...], w2_ref[...]), gpost_ref[...])
    y_ref[...] = y
    if emit_normed:
        u_ref[...] = _rms_norm(y, gnext_ref[...]).astype(BF16)


def _ffn(x, g_pre, w1, w3, w2, g_post, g_next=None):
    m = x.shape[0]
    tm, tf = FFN_UP_ROW_TILE, FFN_FF_TILE
    hidden = pl.pallas_call(
        _ffn_up_kernel,
        grid=(m // tm, D_FF // tf),
        in_specs=[
            pl.BlockSpec((tm, D_MODEL), lambda i, j: (i, 0)),
            pl.BlockSpec((1, D_MODEL), lambda i, j: (0, 0)),
            pl.BlockSpec((D_MODEL, tf), lambda i, j: (0, j)),
            pl.BlockSpec((D_MODEL, tf), lambda i, j: (0, j)),
        ],
        out_specs=pl.BlockSpec((tm, tf), lambda i, j: (i, j)),
        out_shape=jax.ShapeDtypeStruct((m, D_FF), BF16),
        compiler_params=_params("parallel", "arbitrary"),
        name="ffn_up",
    )(x, g_pre, w1, w3)

    tm = FFN_DOWN_ROW_TILE
    emit_normed = g_next is not None
    row_spec = pl.BlockSpec((tm, D_MODEL), lambda i: (i, 0))
    gain_spec = pl.BlockSpec((1, D_MODEL), lambda i: (0, 0))
    in_specs = [pl.BlockSpec((tm, D_FF), lambda i: (i, 0)),
                pl.BlockSpec((D_FF, D_MODEL), lambda i: (0, 0), pipeline_mode=pl.Buffered(1)),
                row_spec, gain_spec]
    args = [hidden, w2, x, FFN_RESIDUAL_WEIGHT * g_post]
    out_specs = [row_spec]
    out_shape = [jax.ShapeDtypeStruct((m, D_MODEL), F32)]
    if emit_normed:
        in_specs.append(gain_spec)
        args.append(g_next)
        out_specs.append(row_spec)
        out_shape.append(jax.ShapeDtypeStruct((m, D_MODEL), BF16))
    outs = pl.pallas_call(
        functools.partial(_ffn_down_kernel, emit_normed=emit_normed),
        grid=(m // tm,),
        in_specs=in_specs,
        out_specs=out_specs,
        out_shape=out_shape,
        compiler_params=_params("parallel"),
        name="ffn_down",
    )(*args)
    return outs if emit_normed else outs[0]


def _proj_kernel(u_ref, w_ref, o_ref):
    o_ref[...] = _dot(u_ref[...], w_ref[...]).astype(BF16)


def _proj(u, w, n):
    b, s, _ = u.shape
    tm, tn = PROJ_ROW_TILE, PROJ_COL_TILE
    return pl.pallas_call(
        _proj_kernel,
        grid=(b, s // tm, n // tn),
        in_specs=[
            pl.BlockSpec((None, tm, D_MODEL), lambda bi, i, j: (bi, i, 0)),
            pl.BlockSpec((D_MODEL, tn), lambda bi, i, j: (0, j)),
        ],
        out_specs=pl.BlockSpec((None, tm, tn), lambda bi, i, j: (bi, i, j)),
        out_shape=jax.ShapeDtypeStruct((b, s, n), BF16),
        compiler_params=_params("parallel", "parallel", "arbitrary"),
        name="proj_d1",
    )(u, w)


def _dilated_proj_kernel(u_ref, w_ref, o_ref, *scratch, dilation):
    slabs = MXU_COLS // LANES
    rows = u_ref.shape[0] // dilation
    n_tiles = w_ref.shape[1] // MXU_COLS
    scratch, staging = scratch[:n_tiles], scratch[n_tiles:]

    def multiply(c):
        res = _dot(u_ref[...], w_ref[:, c * MXU_COLS:(c + 1) * MXU_COLS])
        for k in range(slabs):
            scratch[c][k] = res[:, k * LANES:(k + 1) * LANES]

    def regroup(c):
        for k in range(slabs):
            lo = c * MXU_COLS + k * LANES
            if dilation <= MAX_SUBLANE_STRIDE:
                for r in range(dilation):
                    o_ref[r, :, lo:lo + LANES] = scratch[c][k, pl.ds(r, rows, stride=dilation), :].astype(BF16)
                continue
            inner, outer = MAX_SUBLANE_STRIDE, dilation // MAX_SUBLANE_STRIDE
            tmp_ref = staging[c * slabs + k]
            for r0 in range(inner):
                tmp_ref[r0] = scratch[c][k, pl.ds(r0, rows * outer, stride=inner), :]
            for r0 in range(inner):
                for r1 in range(outer):
                    o_ref[r1 * inner + r0, :, lo:lo + LANES] = (
                        tmp_ref[r0, pl.ds(r1, rows, stride=outer), :].astype(BF16))

    for c in range(n_tiles + 1):
        if c < n_tiles:
            multiply(c)
        if c > 0:
            regroup(c - 1)


def _dilated_proj(u, w, first_col, n, dilation):
    b, s, _ = u.shape
    tm, tn = PROJ_ROW_TILE, PROJ_COL_TILE
    first_tile = first_col // tn
    scratch = [pltpu.VMEM((MXU_COLS // LANES, tm, LANES), F32)] * (tn // MXU_COLS)
    if dilation > MAX_SUBLANE_STRIDE:
        scratch += [pltpu.VMEM((MAX_SUBLANE_STRIDE, tm // MAX_SUBLANE_STRIDE, LANES), F32)] * (tn // LANES)
    return pl.pallas_call(
        functools.partial(_dilated_proj_kernel, dilation=dilation),
        grid=(b, s // tm, n // tn),
        in_specs=[
            pl.BlockSpec((None, tm, D_MODEL), lambda bi, i, j: (bi, i, 0)),
            pl.BlockSpec((D_MODEL, tn), lambda bi, i, j: (0, first_tile + j)),
        ],
        out_specs=pl.BlockSpec((None, dilation, tm // dilation, tn), lambda bi, i, j: (bi, 0, i, j)),
        out_shape=jax.ShapeDtypeStruct((b, dilation, s // dilation, n), BF16),
        scratch_shapes=scratch,
        compiler_params=_params("parallel", "parallel", "arbitrary"),
        name=f"proj_d{dilation}",
    )(u, w)


def _attn_geometry(group, length):
    dilation = ATTN_DILATIONS[group]
    radius = ATTN_WINDOWS[group] // (2 * dilation)
    tq = ATTN_Q_TILE
    tk = min(tq + 2 * radius, length)
    offsets = (0,) if tk == length else (0, radius, tk - tq)
    return dilation, radius, tq, tk, offsets


def _attn_kernel(q_ref, k_ref, v_ref, o_ref, lse_ref, bias_ref, *scratch, group):
    length = q_ref.shape[1]
    dilation, radius, tq, tk, offsets = _attn_geometry(group, length)
    scale = HEAD_DIM ** -0.5 * LOG2_E
    o_nat_ref = o_ref if dilation == 1 else scratch[0]

    @pl.when(pl.program_id(0) == 0)
    def _():
        row = lax.broadcasted_iota(jnp.int32, (tq, tk), 0)
        col = lax.broadcasted_iota(jnp.int32, (tq, tk), 1)
        for vi, off in enumerate(offsets):
            delta = jnp.abs(col - row - off)
            dist = (dilation * delta).astype(F32)
            for h in range(HEADS_PER_GROUP):
                slope = 2.0 ** (-8.0 * (group * HEADS_PER_GROUP + h + 1) / N_HEADS)
                bias_ref[vi * HEADS_PER_GROUP + h] = jnp.where(delta <= radius, -slope * LOG2_E * dist, -jnp.inf)

    def key_start(q0):
        return min(max(q0 - radius, 0), length - tk)

    def scores(r, q0):
        k0 = key_start(q0)
        out = []
        for h in range(HEADS_PER_GROUP):
            cols = pl.ds(h * HEAD_DIM, HEAD_DIM)
            q = q_ref[r, pl.ds(q0, tq), cols]
            k = k_ref[r, pl.ds(k0, tk), cols]
            out.append(lax.dot_general(q, k, (((1,), (1,)), ((), ())), preferred_element_type=F32))
        return out

    def softmax_pv(r, q0, s_heads):
        k0 = key_start(q0)
        vi = offsets.index(q0 - k0)
        out_rows = pl.ds(q0, tq) if dilation == 1 else pl.ds(q0 * dilation + r, tq, stride=dilation)
        for h, s in enumerate(s_heads):
            v = v_ref[r, pl.ds(k0, tk), pl.ds(h * HEAD_DIM, HEAD_DIM)]
            s = s * scale + bias_ref[vi * HEADS_PER_GROUP + h]
            m = jnp.max(s, axis=-1, keepdims=True)
            p = jnp.exp2(s - m)
            l = jnp.sum(p, axis=-1, keepdims=True)
            o = _dot(p.astype(BF16), v) / l
            o_nat_ref[h, out_rows, :] = o.astype(o_nat_ref.dtype)
            lse_ref[h, out_rows, :] = jnp.broadcast_to(m * LN_2 + jnp.log(l), (tq, HEAD_DIM))

    tiles = [(r, it * tq) for r in range(dilation) for it in range(length // tq)]
    in_flight = []
    for idx in range(len(tiles) + ATTN_LOOKAHEAD):
        if idx < len(tiles):
            in_flight.append(scores(*tiles[idx]))
        if idx >= ATTN_LOOKAHEAD:
            softmax_pv(*tiles[idx - ATTN_LOOKAHEAD], in_flight.pop(0))

    if dilation != 1:
        o_ref[...] = o_nat_ref[...].astype(BF16)


def _attn_group(qkv, group, first_col_block):
    b, d, length, _ = qkv.shape
    s = d * length
    _, _, tq, tk, offsets = _attn_geometry(group, length)
    head_major = (HEADS_PER_GROUP, s, HEAD_DIM)

    def in_spec(blk):
        return pl.BlockSpec((None, d, length, D_ATTN), lambda i: (i, 0, 0, first_col_block + blk))

    out_spec = pl.BlockSpec((None,) + head_major, lambda i: (i, 0, 0, 0))
    scratch = [pltpu.VMEM((len(offsets) * HEADS_PER_GROUP, tq, tk), F32)]
    if d != 1:
        scratch.append(pltpu.VMEM(head_major, F32))
    return pl.pallas_call(
        functools.partial(_attn_kernel, group=group),
        grid=(b,),
        in_specs=[in_spec(0), in_spec(1), in_spec(2)],
        out_specs=[out_spec, out_spec],
        out_shape=[jax.ShapeDtypeStruct((b,) + head_major, BF16), jax.ShapeDtypeStruct((b,) + head_major, F32)],
        scratch_shapes=scratch,
        compiler_params=_params("arbitrary"),
        name=f"attn_g{group}",
    )(qkv, qkv, qkv)


def _gated_short_conv(cb_ref, cc_ref, cx_ref, cc_prev_ref, cx_prev_ref, cc_next_ref, cx_next_ref, w_ref, b_ref):
    i = pl.program_id(1)
    tm = cc_ref.shape[0]
    z = cc_ref[...].astype(F32) * cx_ref[...].astype(F32)
    z_before = (cc_prev_ref[...].astype(F32) * cx_prev_ref[...].astype(F32))[HALO_ROWS - 1:HALO_ROWS]
    z_after = (cc_next_ref[...].astype(F32) * cx_next_ref[...].astype(F32))[0:1]
    z_before = jnp.where(i == 0, 0.0, z_before)
    z_after = jnp.where(i == pl.num_programs(1) - 1, 0.0, z_after)
    t = lax.broadcasted_iota(jnp.int32, z.shape, 0)
    z_prev = jnp.where(t == 0, z_before, pltpu.roll(z, 1, 0))
    z_next = jnp.where(t == tm - 1, z_after, pltpu.roll(z, tm - 1, 0))
    w = w_ref[...]
    conv = z_prev * w[0:1] + z * w[1:2] + z_next * w[2:3] + b_ref[...]
    return (cb_ref[...].astype(F32) * conv).astype(BF16)


def _mix_out_kernel(h_ref, u_ref, cb_ref, cc_ref, cx_ref, cc_prev_ref, cx_prev_ref, cc_next_ref, cx_next_ref,
                    o0_ref, o1_ref, o2_ref, l0_ref, l1_ref, l2_ref, cw_ref, cbias_ref,
                    wco_ref, wao_ref, wg_ref, bg_ref, wo_ref, gpost_ref, out_ref, merged_ref):
    yc = _gated_short_conv(cb_ref, cc_ref, cx_ref, cc_prev_ref, cx_prev_ref, cc_next_ref, cx_next_ref,
                           cw_ref, cbias_ref)
    l0, l1, l2 = l0_ref[...], l1_ref[...], l2_ref[...]
    mx = jnp.maximum(jnp.maximum(l0, l1), l2)
    e0, e1, e2 = jnp.exp(l0 - mx), jnp.exp(l1 - mx), jnp.exp(l2 - mx)
    den = e0 + e1 + e2
    o = (e0 * o0_ref[...].astype(F32) + e1 * o1_ref[...].astype(F32) + e2 * o2_ref[...].astype(F32)) / den
    o = o.astype(BF16)
    o = jnp.concatenate([o[hd] for hd in range(HEADS_PER_GROUP)], axis=-1)
    u = u_ref[...]

    tn = MIX_COL_CHUNK
    chunks = [c * tn for c in range(D_MODEL // tn)]
    g_conv = [_sigmoid(_dot(u, wg_ref[:, lo:lo + tn]) + bg_ref[:, lo:lo + tn]) for lo in chunks]
    g_attn = [_sigmoid(_dot(u, wg_ref[:, D_MODEL + lo:D_MODEL + lo + tn]) + bg_ref[:, D_MODEL + lo:D_MODEL + lo + tn])
              for lo in chunks]
    for c, lo in enumerate(chunks):
        y_conv = _dot(yc, wco_ref[:, lo:lo + tn])
        y_attn = _dot(o, wao_ref[:, lo:lo + tn])
        merged_ref[:, lo:lo + tn] = (g_conv[c] * y_conv + g_attn[c] * y_attn).astype(BF16)

    mix = _dot(merged_ref[...], wo_ref[...])
    out_ref[...] = h_ref[...] + _rms_norm(mix, gpost_ref[...])


def _mix_out(h, u, natural, o_groups, lse_groups, conv_w, conv_b, w_conv_out, w_attn_out, w_gate, b_gate, w_o,
             g_post):
    b, s, _ = h.shape
    tm = MIX_ROW_TILE
    halo_blocks_per_tile = tm // HALO_ROWS
    last_halo_block = s // HALO_ROWS - 1

    def rows(width, col_block=0):
        return pl.BlockSpec((None, tm, width), lambda bi, i: (bi, i, col_block))

    def halo_before(col_block):
        return pl.BlockSpec((None, HALO_ROWS, D_CONV),
                            lambda bi, i: (bi, jnp.maximum(i * halo_blocks_per_tile - 1, 0), col_block))

    def halo_after(col_block):
        return pl.BlockSpec((None, HALO_ROWS, D_CONV),
                            lambda bi, i: (bi, jnp.minimum((i + 1) * halo_blocks_per_tile, last_halo_block), col_block))

    heads = pl.BlockSpec((None, HEADS_PER_GROUP, tm, HEAD_DIM), lambda bi, i: (bi, 0, i, 0))

    def resident(shape):
        return pl.BlockSpec(shape, lambda bi, i: (0, 0), pipeline_mode=pl.Buffered(1))

    return pl.pallas_call(
        _mix_out_kernel,
        grid=(b, s // tm),
        in_specs=[
            rows(D_MODEL), rows(D_MODEL),
            rows(D_CONV, 0), rows(D_CONV, 1), rows(D_CONV, 2),
            halo_before(1), halo_before(2), halo_after(1), halo_after(2),
            heads, heads, heads,
            heads, heads, heads,
            resident((CONV_WIDTH, D_CONV)),
            resident((1, D_CONV)),
            resident((D_CONV, D_MODEL)),
            resident((D_ATTN, D_MODEL)),
            resident((D_MODEL, 2 * D_MODEL)),
            resident((1, 2 * D_MODEL)),
            resident((D_MODEL, D_MODEL)),
            resident((1, D_MODEL)),
        ],
        out_specs=rows(D_MODEL),
        out_shape=jax.ShapeDtypeStruct((b, s, D_MODEL), F32),
        scratch_shapes=[pltpu.VMEM((tm, D_MODEL), BF16)],
        compiler_params=_params("parallel", "parallel"),
        name="mix_out",
    )(h, u, natural, natural, natural, natural, natural, natural, natural, *o_groups, *lse_groups,
      conv_w, conv_b, w_conv_out, w_attn_out, w_gate, b_gate, w_o, g_post)


def _encoder_layer(x, p):
    b, s, _ = x.shape
    m = b * s
    h, u = _ffn(x.reshape(m, D_MODEL), p["ffn1_norm_pre"], p["ffn1_w1"], p["ffn1_w3"], p["ffn1_w2"],
                p["ffn1_norm_post"], p["mix_norm_pre"])
    h = h.reshape(b, s, D_MODEL)
    u = u.reshape(b, s, D_MODEL)
    natural = _proj(u, p["w_in"], D_NATURAL)
    o_groups, lse_groups = [], []
    for g in range(N_GROUPS):
        if g == 0:
            o_g, lse_g = _attn_group(natural.reshape(b, 1, s, D_NATURAL), g, 3 * D_CONV // D_ATTN)
        else:
            qkv = _dilated_proj(u, p["w_in"], D_NATURAL + (g - 1) * 3 * D_ATTN, 3 * D_ATTN, ATTN_DILATIONS[g])
            o_g, lse_g = _attn_group(qkv, g, 0)
        o_groups.append(o_g)
        lse_groups.append(lse_g)
    h = _mix_out(h, u, natural, o_groups, lse_groups, p["conv_w"], p["conv_b"], p["w_conv_out"], p["w_attn_out"],
                 p["w_gate"], p["b_gate"], p["w_o"], p["mix_norm_post"])
    y = _ffn(h.reshape(m, D_MODEL), p["ffn2_norm_pre"], p["ffn2_w1"], p["ffn2_w3"], p["ffn2_w2"],
             p["ffn2_norm_post"])
    return y.reshape(b, s, D_MODEL)


_MATMUL_WEIGHTS = ("ffn1_w1", "ffn1_w3", "ffn1_w2", "w_in", "w_conv_out", "w_attn_out", "w_gate", "w_o",
                   "ffn2_w1", "ffn2_w3", "ffn2_w2")
_ROW_VECTORS = ("ffn1_norm_pre", "ffn1_norm_post", "mix_norm_pre", "conv_b", "b_gate", "mix_norm_post",
                "ffn2_norm_pre", "ffn2_norm_post")


def _cast_kernel(x_ref, o_ref):
    o_ref[...] = x_ref[...].astype(BF16)


def _cast_bf16(w):
    rows, cols = w.shape
    tr = rows
    while tr * cols * 4 > CAST_BLOCK_BYTES and tr % (2 * HALO_ROWS) == 0:
        tr //= 2
    spec = pl.BlockSpec((tr, cols), lambda i: (i, 0))
    return pl.pallas_call(
        _cast_kernel, grid=(rows // tr,), in_specs=[spec], out_specs=spec,
        out_shape=jax.ShapeDtypeStruct((rows, cols), BF16),
        compiler_params=_params("parallel"), name="cast_bf16",
    )(w)


def _cast_regroup_w_in(w_in):
    conv_blocks = 3 * D_CONV // D_ATTN

    def source_block(j):
        k = j - conv_blocks
        return jnp.where(j < conv_blocks, j, conv_blocks + (k % 3) * N_GROUPS + k // 3)

    return pl.pallas_call(
        _cast_kernel,
        grid=(D_IN // D_ATTN,),
        in_specs=[pl.BlockSpec((D_MODEL, D_ATTN), lambda j: (0, source_block(j)))],
        out_specs=pl.BlockSpec((D_MODEL, D_ATTN), lambda j: (0, j)),
        out_shape=jax.ShapeDtypeStruct((D_MODEL, D_IN), BF16),
        compiler_params=_params("parallel"), name="cast_w_in",
    )(w_in)


def kernel(x_prompt, x_sample, ffn1_norm_pre, ffn1_w1, ffn1_w3, ffn1_w2, ffn1_norm_post, mix_norm_pre, w_in, conv_w, conv_b, w_conv_out, w_attn_out, w_gate, b_gate, w_o, mix_norm_post, ffn2_norm_pre, ffn2_w1, ffn2_w3, ffn2_w2, ffn2_norm_post):
    stacked = dict(
        ffn1_norm_pre=ffn1_norm_pre, ffn1_w1=ffn1_w1, ffn1_w3=ffn1_w3, ffn1_w2=ffn1_w2,
        ffn1_norm_post=ffn1_norm_post, mix_norm_pre=mix_norm_pre, w_in=w_in, conv_w=conv_w, conv_b=conv_b,
        w_conv_out=w_conv_out, w_attn_out=w_attn_out, w_gate=w_gate, b_gate=b_gate, w_o=w_o,
        mix_norm_post=mix_norm_post, ffn2_norm_pre=ffn2_norm_pre, ffn2_w1=ffn2_w1, ffn2_w3=ffn2_w3,
        ffn2_w2=ffn2_w2, ffn2_norm_post=ffn2_norm_post)
    depth = w_in.shape[0]
    y_prompt, y_sample = x_prompt, x_sample
    for layer in range(depth):
        p = {}
        for name, w in stacked.items():
            w = w[layer]
            if name == "w_in":
                w = _cast_regroup_w_in(w)
            elif name in _MATMUL_WEIGHTS:
                w = _cast_bf16(w)
            elif name in _ROW_VECTORS:
                w = w.reshape(1, -1)
            p[name] = w
        y_prompt = _encoder_layer(y_prompt, p)
        y_sample = _encoder_layer(y_sample, p)
    return (y_prompt, y_sample)
```

```python
import functools

import jax
import jax.numpy as jnp
from jax import lax
from jax.experimental import pallas as pl
from jax.experimental.pallas import tpu as pltpu

D_MODEL = 2048
D_FF = 5632
D_CONV = 1024
CONV_WIDTH = 3
N_GROUPS = 3
HEADS_PER_GROUP = 4
N_HEADS = N_GROUPS * HEADS_PER_GROUP
HEAD_DIM = 128
D_ATTN = HEADS_PER_GROUP * HEAD_DIM
D_QKV = N_GROUPS * D_ATTN
ATTN_WINDOWS = (128, 512, 2048)
ATTN_DILATIONS = (1, 4, 16)
D_IN = 3 * D_CONV + 3 * D_QKV
D_NATURAL = 3 * D_CONV + 3 * D_ATTN
RMS_EPS = 1e-6
FFN_RESIDUAL_WEIGHT = 0.5
LOG2_E = 1.4426950408889634
LN_2 = 0.6931471805599453

V7X_VMEM_BYTES = 64 * 1024 * 1024
VMEM_LIMIT_BYTES = V7X_VMEM_BYTES - 8 * 1024 * 1024

FFN_UP_ROW_TILE = 1024
FFN_DOWN_ROW_TILE = 256
FFN_FF_TILE = 512
PROJ_ROW_TILE = 1024
PROJ_COL_TILE = 1536
MIX_ROW_TILE = 256
MIX_COL_CHUNK = 2048
CAST_BLOCK_BYTES = 4 * 1024 * 1024
ATTN_Q_TILE = 128
ATTN_LOOKAHEAD = 1

LANES = 128
MXU_COLS = 256
HALO_ROWS = 16
MAX_SUBLANE_STRIDE = 4

BF16 = jnp.bfloat16
F32 = jnp.float32


def _rms_norm(x, g):
    return x * lax.rsqrt(jnp.mean(x * x, axis=-1, keepdims=True) + RMS_EPS) * g


def _dot(a, b):
    return jnp.dot(a, b, preferred_element_type=F32)


def _sigmoid(x):
    return 0.5 * jnp.tanh(0.5 * x) + 0.5


def _params(*semantics):
    return pltpu.CompilerParams(dimension_semantics=semantics, vmem_limit_bytes=VMEM_LIMIT_BYTES)


def _ffn_up_kernel(x_ref, gpre_ref, w1_ref, w3_ref, hidden_ref):
    x = x_ref[...]
    rinv = lax.rsqrt(jnp.mean(x * x, axis=-1, keepdims=True) + RMS_EPS)
    xg = (x * gpre_ref[...]).astype(BF16)
    a = _dot(xg, w1_ref[...]) * rinv
    c = _dot(xg, w3_ref[...]) * rinv
    hidden_ref[...] = (a * _sigmoid(a) * c).astype(BF16)


def _ffn_down_kernel(*refs, emit_normed):
    if emit_normed:
        hidden_ref, w2_ref, x_ref, gpost_ref, gnext_ref, y_ref, u_ref = refs
    else:
        hidden_ref, w2_ref, x_ref, gpost_ref, y_ref = refs
    y = x_ref[...] + _rms_norm(_dot(hidden_ref[...], w2_ref[...]), gpost_ref[...])
    y_ref[...] = y
    if emit_normed:
        u_ref[...] = _rms_norm(y, gnext_ref[...]).astype(BF16)


def _ffn(x, g_pre, w1, w3, w2, g_post, g_next=None):
    m = x.shape[0]
    tm, tf = FFN_UP_ROW_TILE, FFN_FF_TILE
    hidden = pl.pallas_call(
        _ffn_up_kernel,
        grid=(m // tm, D_FF // tf),
        in_specs=[
            pl.BlockSpec((tm, D_MODEL), lambda i, j: (i, 0)),
            pl.BlockSpec((1, D_MODEL), lambda i, j: (0, 0)),
            pl.BlockSpec((D_MODEL, tf), lambda i, j: (0, j)),
            pl.BlockSpec((D_MODEL, tf), lambda i, j: (0, j)),
        ],
        out_specs=pl.BlockSpec((tm, tf), lambda i, j: (i, j)),
        out_shape=jax.ShapeDtypeStruct((m, D_FF), BF16),
        compiler_params=_params("parallel", "arbitrary"),
        name="ffn_up",
    )(x, g_pre, w1, w3)

    tm = FFN_DOWN_ROW_TILE
    emit_normed = g_next is not None
    row_spec = pl.BlockSpec((tm, D_MODEL), lambda i: (i, 0))
    gain_spec = pl.BlockSpec((1, D_MODEL), lambda i: (0, 0))
    in_specs = [pl.BlockSpec((tm, D_FF), lambda i: (i, 0)),
                pl.BlockSpec((D_FF, D_MODEL), lambda i: (0, 0), pipeline_mode=pl.Buffered(1)),
                row_spec, gain_spec]
    args = [hidden, w2, x, FFN_RESIDUAL_WEIGHT * g_post]
    out_specs = [row_spec]
    out_shape = [jax.ShapeDtypeStruct((m, D_MODEL), F32)]
    if emit_normed:
        in_specs.append(gain_spec)
        args.append(g_next)
        out_specs.append(row_spec)
        out_shape.append(jax.ShapeDtypeStruct((m, D_MODEL), BF16))
    outs = pl.pallas_call(
        functools.partial(_ffn_down_kernel, emit_normed=emit_normed),
        grid=(m // tm,),
        in_specs=in_specs,
        out_specs=out_specs,
        out_shape=out_shape,
        compiler_params=_params("parallel"),
        name="ffn_down",
    )(*args)
    return outs if emit_normed else outs[0]


def _proj_kernel(u_ref, w_ref, o_ref):
    o_ref[...] = _dot(u_ref[...], w_ref[...]).astype(BF16)


def _proj(u, w, n):
    b, s, _ = u.shape
    tm, tn = PROJ_ROW_TILE, PROJ_COL_TILE
    return pl.pallas_call(
        _proj_kernel,
        grid=(b, s // tm, n // tn),
        in_specs=[
            pl.BlockSpec((None, tm, D_MODEL), lambda bi, i, j: (bi, i, 0)),
            pl.BlockSpec((D_MODEL, tn), lambda bi, i, j: (0, j)),
        ],
        out_specs=pl.BlockSpec((None, tm, tn), lambda bi, i, j: (bi, i, j)),
        out_shape=jax.ShapeDtypeStruct((b, s, n), BF16),
        compiler_params=_params("parallel", "parallel", "arbitrary"),
        name="proj_d1",
    )(u, w)


def _dilated_proj_kernel(u_ref, w_ref, o_ref, *scratch, dilation):
    slabs = MXU_COLS // LANES
    rows = u_ref.shape[0] // dilation
    n_tiles = w_ref.shape[1] // MXU_COLS
    scratch, staging = scratch[:n_tiles], scratch[n_tiles:]

    def multiply(c):
        res = _dot(u_ref[...], w_ref[:, c * MXU_COLS:(c + 1) * MXU_COLS])
        for k in range(slabs):
            scratch[c][k] = res[:, k * LANES:(k + 1) * LANES]

    def regroup(c):
        for k in range(slabs):
            lo = c * MXU_COLS + k * LANES
            if dilation <= MAX_SUBLANE_STRIDE:
                for r in range(dilation):
                    o_ref[r, :, lo:lo + LANES] = scratch[c][k, pl.ds(r, rows, stride=dilation), :].astype(BF16)
                continue
            inner, outer = MAX_SUBLANE_STRIDE, dilation // MAX_SUBLANE_STRIDE
            tmp_ref = staging[c * slabs + k]
            for r0 in range(inner):
                tmp_ref[r0] = scratch[c][k, pl.ds(r0, rows * outer, stride=inner), :]
            for r0 in range(inner):
                for r1 in range(outer):
                    o_ref[r1 * inner + r0, :, lo:lo + LANES] = (
                        tmp_ref[r0, pl.ds(r1, rows, stride=outer), :].astype(BF16))

    for c in range(n_tiles + 1):
        if c < n_tiles:
            multiply(c)
        if c > 0:
            regroup(c - 1)


def _dilated_proj(u, w, first_col, n, dilation):
    b, s, _ = u.shape
    tm, tn = PROJ_ROW_TILE, PROJ_COL_TILE
    first_tile = first_col // tn
    scratch = [pltpu.VMEM((MXU_COLS // LANES, tm, LANES), F32)] * (tn // MXU_COLS)
    if dilation > MAX_SUBLANE_STRIDE:
        scratch += [pltpu.VMEM((MAX_SUBLANE_STRIDE, tm // MAX_SUBLANE_STRIDE, LANES), F32)] * (tn // LANES)
    return pl.pallas_call(
        functools.partial(_dilated_proj_kernel, dilation=dilation),
        grid=(b, s // tm, n // tn),
        in_specs=[
            pl.BlockSpec((None, tm, D_MODEL), lambda bi, i, j: (bi, i, 0)),
            pl.BlockSpec((D_MODEL, tn), lambda bi, i, j: (0, first_tile + j)),
        ],
        out_specs=pl.BlockSpec((None, dilation, tm // dilation, tn), lambda bi, i, j: (bi, 0, i, j)),
        out_shape=jax.ShapeDtypeStruct((b, dilation, s // dilation, n), BF16),
        scratch_shapes=scratch,
        compiler_params=_params("parallel", "parallel", "arbitrary"),
        name=f"proj_d{dilation}",
    )(u, w)


def _attn_geometry(group, length):
    dilation = ATTN_DILATIONS[group]
    radius = ATTN_WINDOWS[group] // (2 * dilation)
    tq = ATTN_Q_TILE
    tk = min(tq + 2 * radius, length)
    offsets = (0,) if tk == length else (0, radius, tk - tq)
    return dilation, radius, tq, tk, offsets


def _attn_kernel(q_ref, k_ref, v_ref, o_ref, lse_ref, bias_ref, *scratch, group):
    length = q_ref.shape[1]
    dilation, radius, tq, tk, offsets = _attn_geometry(group, length)
    scale = HEAD_DIM ** -0.5 * LOG2_E
    o_nat_ref = o_ref if dilation == 1 else scratch[0]

    @pl.when(pl.program_id(0) == 0)
    def _():
        row = lax.broadcasted_iota(jnp.int32, (tq, tk), 0)
        col = lax.broadcasted_iota(jnp.int32, (tq, tk), 1)
        for vi, off in enumerate(offsets):
            delta = jnp.abs(col - row - off)
            dist = (dilation * delta).astype(F32)
            for h in range(HEADS_PER_GROUP):
                slope = 2.0 ** (-8.0 * (group * HEADS_PER_GROUP + h + 1) / N_HEADS)
                bias_ref[vi * HEADS_PER_GROUP + h] = jnp.where(delta <= radius, -slope * LOG2_E * dist, -jnp.inf)

    def key_start(q0):
        return min(max(q0 - radius, 0), length - tk)

    def scores(r, q0):
        k0 = key_start(q0)
        out = []
        for h in range(HEADS_PER_GROUP):
            cols = pl.ds(h * HEAD_DIM, HEAD_DIM)
            q = q_ref[r, pl.ds(q0, tq), cols]
            k = k_ref[r, pl.ds(k0, tk), cols]
            out.append(lax.dot_general(q, k, (((1,), (1,)), ((), ())), preferred_element_type=F32))
        return out

    def softmax_pv(r, q0, s_heads):
        k0 = key_start(q0)
        vi = offsets.index(q0 - k0)
        out_rows = pl.ds(q0, tq) if dilation == 1 else pl.ds(q0 * dilation + r, tq, stride=dilation)
        for h, s in enumerate(s_heads):
            v = v_ref[r, pl.ds(k0, tk), pl.ds(h * HEAD_DIM, HEAD_DIM)]
            s = s * scale + bias_ref[vi * HEADS_PER_GROUP + h]
            m = jnp.max(s, axis=-1, keepdims=True)
            p = jnp.exp2(s - m)
            l = jnp.sum(p, axis=-1, keepdims=True)
            o = _dot(p.astype(BF16), v) / l
            o_nat_ref[h, out_rows, :] = o.astype(o_nat_ref.dtype)
            lse_ref[h, out_rows, :] = jnp.broadcast_to(m * LN_2 + jnp.log(l), (tq, HEAD_DIM))

    tiles = [(r, it * tq) for r in range(dilation) for it in range(length // tq)]
    in_flight = []
    for idx in range(len(tiles) + ATTN_LOOKAHEAD):
        if idx < len(tiles):
            in_flight.append(scores(*tiles[idx]))
        if idx >= ATTN_LOOKAHEAD:
            softmax_pv(*tiles[idx - ATTN_LOOKAHEAD], in_flight.pop(0))

    if dilation != 1:
        o_ref[...] = o_nat_ref[...].astype(BF16)


def _attn_group(qkv, group, first_col_block):
    b, d, length, _ = qkv.shape
    s = d * length
    _, _, tq, tk, offsets = _attn_geometry(group, length)
    head_major = (HEADS_PER_GROUP, s, HEAD_DIM)

    def in_spec(blk):
        return pl.BlockSpec((None, d, length, D_ATTN), lambda i: (i, 0, 0, first_col_block + blk))

    out_spec = pl.BlockSpec((None,) + head_major, lambda i: (i, 0, 0, 0))
    scratch = [pltpu.VMEM((len(offsets) * HEADS_PER_GROUP, tq, tk), F32)]
    if d != 1:
        scratch.append(pltpu.VMEM(head_major, F32))
    return pl.pallas_call(
        functools.partial(_attn_kernel, group=group),
        grid=(b,),
        in_specs=[in_spec(0), in_spec(1), in_spec(2)],
        out_specs=[out_spec, out_spec],
        out_shape=[jax.ShapeDtypeStruct((b,) + head_major, BF16), jax.ShapeDtypeStruct((b,) + head_major, F32)],
        scratch_shapes=scratch,
        compiler_params=_params("arbitrary"),
        name=f"attn_g{group}",
    )(qkv, qkv, qkv)


def _gated_short_conv(cb_ref, cc_ref, cx_ref, cc_prev_ref, cx_prev_ref, cc_next_ref, cx_next_ref, w_ref, b_ref):
    i = pl.program_id(1)
    tm = cc_ref.shape[0]
    z = cc_ref[...].astype(F32) * cx_ref[...].astype(F32)
    z_before = (cc_prev_ref[...].astype(F32) * cx_prev_ref[...].astype(F32))[HALO_ROWS - 1:HALO_ROWS]
    z_after = (cc_next_ref[...].astype(F32) * cx_next_ref[...].astype(F32))[0:1]
    z_before = jnp.where(i == 0, 0.0, z_before)
    z_after = jnp.where(i == pl.num_programs(1) - 1, 0.0, z_after)
    t = lax.broadcasted_iota(jnp.int32, z.shape, 0)
    z_prev = jnp.where(t == 0, z_before, pltpu.roll(z, 1, 0))
    z_next = jnp.where(t == tm - 1, z_after, pltpu.roll(z, tm - 1, 0))
    w = w_ref[...]
    conv = z_prev * w[0:1] + z * w[1:2] + z_next * w[2:3] + b_ref[...]
    return (cb_ref[...].astype(F32) * conv).astype(BF16)


def _mix_out_kernel(h_ref, u_ref, cb_ref, cc_ref, cx_ref, cc_prev_ref, cx_prev_ref, cc_next_ref, cx_next_ref,
                    o0_ref, o1_ref, o2_ref, l0_ref, l1_ref, l2_ref, cw_ref, cbias_ref,
                    wco_ref, wao_ref, wg_ref, bg_ref, wo_ref, gpost_ref, out_ref, merged_ref):
    yc = _gated_short_conv(cb_ref, cc_ref, cx_ref, cc_prev_ref, cx_prev_ref, cc_next_ref, cx_next_ref,
                           cw_ref, cbias_ref)
    l0, l1, l2 = l0_ref[...], l1_ref[...], l2_ref[...]
    mx = jnp.maximum(jnp.maximum(l0, l1), l2)
    e0, e1, e2 = jnp.exp(l0 - mx), jnp.exp(l1 - mx), jnp.exp(l2 - mx)
    den = e0 + e1 + e2
    o = (e0 * o0_ref[...].astype(F32) + e1 * o1_ref[...].astype(F32) + e2 * o2_ref[...].astype(F32)) / den
    o = o.astype(BF16)
    o = jnp.concatenate([o[hd] for hd in range(HEADS_PER_GROUP)], axis=-1)
    u = u_ref[...]

    tn = MIX_COL_CHUNK
    chunks = [c * tn for c in range(D_MODEL // tn)]
    g_conv = [_sigmoid(_dot(u, wg_ref[:, lo:lo + tn]) + bg_ref[:, lo:lo + tn]) for lo in chunks]
    g_attn = [_sigmoid(_dot(u, wg_ref[:, D_MODEL + lo:D_MODEL + lo + tn]) + bg_ref[:, D_MODEL + lo:D_MODEL + lo + tn])
              for lo in chunks]
    for c, lo in enumerate(chunks):
        y_conv = _dot(yc, wco_ref[:, lo:lo + tn])
        y_attn = _dot(o, wao_ref[:, lo:lo + tn])
        merged_ref[:, lo:lo + tn] = (g_conv[c] * y_conv + g_attn[c] * y_attn).astype(BF16)

    mix = _dot(merged_ref[...], wo_ref[...])
    out_ref[...] = h_ref[...] + _rms_norm(mix, gpost_ref[...])


def _mix_out(h, u, natural, o_groups, lse_groups, conv_w, conv_b, w_conv_out, w_attn_out, w_gate, b_gate, w_o,
             g_post):
    b, s, _ = h.shape
    tm = MIX_ROW_TILE
    halo_blocks_per_tile = tm // HALO_ROWS
    last_halo_block = s // HALO_ROWS - 1

    def rows(width, col_block=0):
        return pl.BlockSpec((None, tm, width), lambda bi, i: (bi, i, col_block))

    def halo_before(col_block):
        return pl.BlockSpec((None, HALO_ROWS, D_CONV),
                            lambda bi, i: (bi, jnp.maximum(i * halo_blocks_per_tile - 1, 0), col_block))

    def halo_after(col_block):
        return pl.BlockSpec((None, HALO_ROWS, D_CONV),
                            lambda bi, i: (bi, jnp.minimum((i + 1) * halo_blocks_per_tile, last_halo_block), col_block))

    heads = pl.BlockSpec((None, HEADS_PER_GROUP, tm, HEAD_DIM), lambda bi, i: (bi, 0, i, 0))

    def resident(shape):
        return pl.BlockSpec(shape, lambda bi, i: (0, 0), pipeline_mode=pl.Buffered(1))

    return pl.pallas_call(
        _mix_out_kernel,
        grid=(b, s // tm),
        in_specs=[
            rows(D_MODEL), rows(D_MODEL),
            rows(D_CONV, 0), rows(D_CONV, 1), rows(D_CONV, 2),
            halo_before(1), halo_before(2), halo_after(1), halo_after(2),
            heads, heads, heads,
            heads, heads, heads,
            resident((CONV_WIDTH, D_CONV)),
            resident((1, D_CONV)),
            resident((D_CONV, D_MODEL)),
            resident((D_ATTN, D_MODEL)),
            resident((D_MODEL, 2 * D_MODEL)),
            resident((1, 2 * D_MODEL)),
            resident((D_MODEL, D_MODEL)),
            resident((1, D_MODEL)),
        ],
        out_specs=rows(D_MODEL),
        out_shape=jax.ShapeDtypeStruct((b, s, D_MODEL), F32),
        scratch_shapes=[pltpu.VMEM((tm, D_MODEL), BF16)],
        compiler_params=_params("parallel", "parallel"),
        name="mix_out",
    )(h, u, natural, natural, natural, natural, natural, natural, natural, *o_groups, *lse_groups,
      conv_w, conv_b, w_conv_out, w_attn_out, w_gate, b_gate, w_o, g_post)


def _encoder_layer(x, p):
    b, s, _ = x.shape
    m = b * s
    h, u = _ffn(x.reshape(m, D_MODEL), p["ffn1_norm_pre"], p["ffn1_w1"], p["ffn1_w3"], p["ffn1_w2"],
                p["ffn1_norm_post"], p["mix_norm_pre"])
    h = h.reshape(b, s, D_MODEL)
    u = u.reshape(b, s, D_MODEL)
    natural = _proj(u, p["w_in"], D_NATURAL)
    o_groups, lse_groups = [], []
    for g in range(N_GROUPS):
        if g == 0:
            o_g, lse_g = _attn_group(natural.reshape(b, 1, s, D_NATURAL), g, 3 * D_CONV // D_ATTN)
        else:
            qkv = _dilated_proj(u, p["w_in"], D_NATURAL + (g - 1) * 3 * D_ATTN, 3 * D_ATTN, ATTN_DILATIONS[g])
            o_g, lse_g = _attn_group(qkv, g, 0)
        o_groups.append(o_g)
        lse_groups.append(lse_g)
    h = _mix_out(h, u, natural, o_groups, lse_groups, p["conv_w"], p["conv_b"], p["w_conv_out"], p["w_attn_out"],
                 p["w_gate"], p["b_gate"], p["w_o"], p["mix_norm_post"])
    y = _ffn(h.reshape(m, D_MODEL), p["ffn2_norm_pre"], p["ffn2_w1"], p["ffn2_w3"], p["ffn2_w2"],
             p["ffn2_norm_post"])
    return y.reshape(b, s, D_MODEL)


_MATMUL_WEIGHTS = ("ffn1_w1", "ffn1_w3", "ffn1_w2", "w_in", "w_conv_out", "w_attn_out", "w_gate", "w_o",
                   "ffn2_w1", "ffn2_w3", "ffn2_w2")
_ROW_VECTORS = ("ffn1_norm_pre", "ffn1_norm_post", "mix_norm_pre", "conv_b", "b_gate", "mix_norm_post",
                "ffn2_norm_pre", "ffn2_norm_post")


def _cast_kernel(x_ref, o_ref):
    o_ref[...] = x_ref[...].astype(BF16)


def _cast_bf16(w):
    rows, cols = w.shape
    tr = rows
    while tr * cols * 4 > CAST_BLOCK_BYTES and tr % (2 * HALO_ROWS) == 0:
        tr //= 2
    spec = pl.BlockSpec((tr, cols), lambda i: (i, 0))
    return pl.pallas_call(
        _cast_kernel, grid=(rows // tr,), in_specs=[spec], out_specs=spec,
        out_shape=jax.ShapeDtypeStruct((rows, cols), BF16),
        compiler_params=_params("parallel"), name="cast_bf16",
    )(w)


def _cast_regroup_w_in(w_in):
    conv_blocks = 3 * D_CONV // D_ATTN

    def source_block(j):
        k = j - conv_blocks
        return jnp.where(j < conv_blocks, j, conv_blocks + (k % 3) * N_GROUPS + k // 3)

    return pl.pallas_call(
        _cast_kernel,
        grid=(D_IN // D_ATTN,),
        in_specs=[pl.BlockSpec((D_MODEL, D_ATTN), lambda j: (0, source_block(j)))],
        out_specs=pl.BlockSpec((D_MODEL, D_ATTN), lambda j: (0, j)),
        out_shape=jax.ShapeDtypeStruct((D_MODEL, D_IN), BF16),
        compiler_params=_params("parallel"), name="cast_w_in",
    )(w_in)


def kernel(x_prompt, x_sample, ffn1_norm_pre, ffn1_w1, ffn1_w3, ffn1_w2, ffn1_norm_post, mix_norm_pre, w_in, conv_w, conv_b, w_conv_out, w_attn_out, w_gate, b_gate, w_o, mix_norm_post, ffn2_norm_pre, ffn2_w1, ffn2_w3, ffn2_w2, ffn2_norm_post):
    stacked = dict(
        ffn1_norm_pre=ffn1_norm_pre, ffn1_w1=ffn1_w1, ffn1_w3=ffn1_w3, ffn1_w2=ffn1_w2,
        ffn1_norm_post=ffn1_norm_post, mix_norm_pre=mix_norm_pre, w_in=w_in, conv_w=conv_w, conv_b=conv_b,
        w_conv_out=w_conv_out, w_attn_out=w_attn_out, w_gate=w_gate, b_gate=b_gate, w_o=w_o,
        mix_norm_post=mix_norm_post, ffn2_norm_pre=ffn2_norm_pre, ffn2_w1=ffn2_w1, ffn2_w3=ffn2_w3,
        ffn2_w2=ffn2_w2, ffn2_norm_post=ffn2_norm_post)
    depth = w_in.shape[0]
    y_prompt, y_sample = x_prompt, x_sample
    for layer in range(depth):
        p = {}
        for name, w in stacked.items():
            w = w[layer]
            if name == "w_in":
                w = _cast_regroup_w_in(w)
            elif name in _MATMUL_WEIGHTS:
                w = _cast_bf16(w)
            elif name in _ROW_VECTORS:
                w = w.reshape(1, -1)
            p[name] = w
        y_prompt = _encoder_layer(y_prompt, p)
        y_sample = _encoder_layer(y_sample, p)
    return (y_prompt, y_sample)
```
